```python
import math
import jax, jax.numpy as jnp
from jax import lax
import numpy as np

D_MODEL = 2048
BATCH = 4
SEQ = 2048
DEPTH = 4
DEC_BATCH = 32
DEC_SEQ = 4
PAST_LEN = 16384
PAGE_SIZE = 128

N_MIXERS = 2
N_A = (DEPTH + 1) // 2
N_B = DEPTH // 2
HEAD_DIM = 64
N_HEADS = D_MODEL // HEAD_DIM
N_KV = N_HEADS // 8
GROUP = N_HEADS // N_KV
WINDOW = 128
BLK = WINDOW
CACHE_W = min(WINDOW, PAST_LEN)
ROPE_THETA = 10000.0
ATTN_SCALE = HEAD_DIM ** -0.5
GLA_HEADS = 4
GLA_DK = D_MODEL // 2 // GLA_HEADS
GLA_DV = D_MODEL // GLA_HEADS
GLA_RANK = 16
GLA_TAU = 16.0
GLA_CHUNK = 32
D_FF = 4 * D_MODEL
EPS = 1e-6

kernel_name = 'hybrid_swa_sink_gla_decoder_step'


def rms_norm(x, g):
    xf = x.astype(jnp.float32)
    y = xf * lax.rsqrt(jnp.mean(xf * xf, axis=-1, keepdims=True) + EPS)
    return (y * g.astype(jnp.float32)).astype(x.dtype)


def rope(x, pos):
    half = HEAD_DIM // 2
    inv = ROPE_THETA ** (-jnp.arange(half, dtype=jnp.float32) * 2.0 / HEAD_DIM)
    ang = pos[:, None] * inv[None, :]
    cos = jnp.cos(ang)[:, None, :]
    sin = jnp.sin(ang)[:, None, :]
    xf = x.astype(jnp.float32)
    x1, x2 = xf[..., :half], xf[..., half:]
    return jnp.concatenate([x1 * cos - x2 * sin, x2 * cos + x1 * sin], axis=-1).astype(x.dtype)


def sink_attend(q, k, v, mask, sink):
    s = jnp.einsum('...qkgd,...skd->...kgqs', q, k).astype(jnp.float32) * ATTN_SCALE
    s = jnp.where(mask, s, -jnp.inf)
    sk = sink.astype(jnp.float32)[:, :, None, None]
    m = jnp.maximum(jnp.max(s, axis=-1, keepdims=True), sk)
    p = jnp.exp(s - m)
    denom = jnp.sum(p, axis=-1, keepdims=True) + jnp.exp(sk - m)
    return jnp.einsum('...kgqs,...skd->...qkgd', (p / denom).astype(v.dtype), v)


def swa_project(h, w_qkv, pos):
    b, l, _ = h.shape
    qkv = h @ w_qkv
    nq, nk = N_HEADS * HEAD_DIM, N_KV * HEAD_DIM
    q = rope(qkv[..., :nq].reshape(b, l, N_HEADS, HEAD_DIM), pos)
    k = rope(qkv[..., nq:nq + nk].reshape(b, l, N_KV, HEAD_DIM), pos)
    v = qkv[..., nq + nk:].reshape(b, l, N_KV, HEAD_DIM)
    return q.reshape(b, l, N_KV, GROUP, HEAD_DIM), k, v


def swa_prompt(h, w_qkv, sink, w_o):
    b, s, _ = h.shape
    nb = s // BLK
    q, k, v = swa_project(h, w_qkv, jnp.arange(s, dtype=jnp.float32))
    qb = q.reshape(b, nb, BLK, N_KV, GROUP, HEAD_DIM)

    def with_prev(t):
        prev = jnp.concatenate([jnp.zeros_like(t[:, :1]), t[:, :-1]], axis=1)
        return jnp.concatenate([prev, t], axis=2)

    kk = with_prev(k.reshape(b, nb, BLK, N_KV, HEAD_DIM))
    vv = with_prev(v.reshape(b, nb, BLK, N_KV, HEAD_DIM))
    q_rel = BLK + jnp.arange(BLK)
    k_rel = jnp.arange(2 * BLK)
    diff = q_rel[:, None] - k_rel[None, :]
    band = (diff >= 0) & (diff < WINDOW)
    k_abs = jnp.arange(nb)[:, None] * BLK - BLK + k_rel[None, :]
    mask = band[None] & (k_abs >= 0)[:, None, :]
    o = sink_attend(qb, kk, vv, mask[:, None, None], sink.reshape(N_KV, GROUP))
    y = o.reshape(b, s, N_HEADS * HEAD_DIM) @ w_o
    return y, k[:, s - CACHE_W:], v[:, s - CACHE_W:]


def swa_sample(h, ck, cv, w_qkv, sink, w_o):
    b, l, _ = h.shape
    pos = PAST_LEN + jnp.arange(l)
    q, k, v = swa_project(h, w_qkv, pos.astype(jnp.float32))
    kk = jnp.concatenate([ck.astype(k.dtype), k], axis=1)
    vv = jnp.concatenate([cv.astype(v.dtype), v], axis=1)
    kpos = jnp.concatenate([PAST_LEN - CACHE_W + jnp.arange(CACHE_W), pos])
    diff = pos[:, None] - kpos[None, :]
    mask = (diff >= 0) & (diff < WINDOW)
    o = sink_attend(q, kk, vv, mask, sink.reshape(N_KV, GROUP))
    y = o.reshape(b, l, N_HEADS * HEAD_DIM) @ w_o
    return y, kk[:, -CACHE_W:], vv[:, -CACHE_W:]


def gla_recurrence(q, k, v, log_a, s0):
    b, l, h, _ = q.shape
    c = math.gcd(l, GLA_CHUNK)
    n = l // c

    def chunks(t):
        return jnp.moveaxis(t.astype(jnp.float32).reshape(b, n, c, *t.shape[2:]), 1, 0)

    causal = jnp.tril(jnp.ones((c, c), dtype=bool))

    def step(state, inp):
        qc, kc, vc, gc = inp
        cum = jnp.cumsum(gc, axis=1)
        last = cum[:, -1]
        qe = qc * jnp.exp(cum)
        ke = kc * jnp.exp(-cum)
        att = jnp.einsum('bqhd,bshd->bhqs', qe, ke)
        att = jnp.where(causal, att, 0.0)
        o = jnp.einsum('bhqs,bshv->bqhv', att, vc) + jnp.einsum('bqhd,bhdv->bqhv', qe, state)
        state = jnp.exp(last)[..., None] * state + jnp.einsum(
            'bshd,bshv->bhdv', kc * jnp.exp(last[:, None] - cum), vc)
        return state, o

    s_final, o = lax.scan(step, s0, (chunks(q), chunks(k), chunks(v), chunks(log_a)))
    o = jnp.moveaxis(o, 0, 1).reshape(b, l, h, GLA_DV)
    return o, s_final


def gla_mixer(h, s0, w_in, w_a1, w_a2, b_a, g_norm, w_o):
    b, l, _ = h.shape
    hk, hv = GLA_HEADS * GLA_DK, GLA_HEADS * GLA_DV
    proj = h @ w_in
    q = proj[..., :hk].reshape(b, l, GLA_HEADS, GLA_DK) * (GLA_DK ** -0.5)
    k = proj[..., hk:2 * hk].reshape(b, l, GLA_HEADS, GLA_DK)
    v = proj[..., 2 * hk:2 * hk + hv].reshape(b, l, GLA_HEADS, GLA_DV)
    g = proj[..., 2 * hk + hv:]
    z = ((h @ w_a1) @ w_a2 + b_a).astype(jnp.float32)
    log_a = (jax.nn.log_sigmoid(z) / GLA_TAU).reshape(b, l, GLA_HEADS, GLA_DK)
    o, s_final = gla_recurrence(q, k, v, log_a, s0)
    on = rms_norm(o.astype(h.dtype), g_norm).reshape(b, l, hv)
    y = (on * jax.nn.silu(g)) @ w_o
    return y, s_final


def mlp(x, g, w_up, w_down):
    h = rms_norm(x, g)
    return jnp.square(jax.nn.relu(h @ w_up)) @ w_down


def setup_inputs(seed: int = 0) -> dict:
    key = jax.random.key(seed)
    ks = jax.random.split(key, 20)
    f32 = jnp.float32

    def nrm(k, shape, scale):
        return jax.random.normal(k, shape, f32) * scale

    hk, hv = GLA_HEADS * GLA_DK, GLA_HEADS * GLA_DV
    return {
        'x_prompt': nrm(ks[0], (BATCH, SEQ, D_MODEL), 1.0),
        'x_sample': nrm(ks[1], (DEC_BATCH, DEC_SEQ, D_MODEL), 1.0),
        'cache_swa_k': nrm(ks[2], (N_A, DEC_BATCH, CACHE_W, N_KV, HEAD_DIM), 1.0),
        'cache_swa_v': nrm(ks[3], (N_A, DEC_BATCH, CACHE_W, N_KV, HEAD_DIM), 1.0),
        'state_gla': nrm(ks[4], (N_B, DEC_BATCH, GLA_HEADS, GLA_DK, GLA_DV), 0.1),
        'norm_attn': 1.0 + nrm(ks[5], (DEPTH, D_MODEL), 0.02),
        'norm_mlp': 1.0 + nrm(ks[6], (DEPTH, D_MODEL), 0.02),
        'norm_final': 1.0 + nrm(ks[7], (D_MODEL,), 0.02),
        'swa_w_qkv': nrm(ks[8], (N_A, D_MODEL, (N_HEADS + 2 * N_KV) * HEAD_DIM), D_MODEL ** -0.5),
        'swa_sink': nrm(ks[9], (N_A, N_HEADS), 0.5),
        'swa_w_o': nrm(ks[10], (N_A, N_HEADS * HEAD_DIM, D_MODEL), (N_HEADS * HEAD_DIM) ** -0.5),
        'gla_w_in': nrm(ks[11], (N_B, D_MODEL, 2 * hk + 2 * hv), D_MODEL ** -0.5),
        'gla_w_a1': nrm(ks[12], (N_B, D_MODEL, GLA_RANK), D_MODEL ** -0.5),
        'gla_w_a2': nrm(ks[13], (N_B, GLA_RANK, hk), GLA_RANK ** -0.5),
        'gla_b_a': nrm(ks[14], (N_B, hk), 0.1),
        'gla_norm': 1.0 + nrm(ks[15], (N_B, GLA_DV), 0.02),
        'gla_w_o': nrm(ks[16], (N_B, hv, D_MODEL), hv ** -0.5),
        'mlp_w_up': nrm(ks[17], (DEPTH, D_MODEL, D_FF), D_MODEL ** -0.5),
        'mlp_w_down': nrm(ks[18], (DEPTH, D_FF, D_MODEL), D_FF ** -0.5),
    }


def reference(x_prompt, x_sample, cache_swa_k, cache_swa_v, state_gla, norm_attn, norm_mlp,
              norm_final, swa_w_qkv, swa_sink, swa_w_o, gla_w_in, gla_w_a1, gla_w_a2, gla_b_a,
              gla_norm, gla_w_o, mlp_w_up, mlp_w_down):
    xp, xs = x_prompt, x_sample
    kp_l, vp_l, sp_l, ks_l, vs_l, ss_l = [], [], [], [], [], []
    for i in range(DEPTH):
        j = i // N_MIXERS
        hp = rms_norm(xp, norm_attn[i])
        hs = rms_norm(xs, norm_attn[i])
        if i % N_MIXERS == 0:
            yp, kp, vp = swa_prompt(hp, swa_w_qkv[j], swa_sink[j], swa_w_o[j])
            ys, kn, vn = swa_sample(hs, cache_swa_k[j], cache_swa_v[j], swa_w_qkv[j], swa_sink[j], swa_w_o[j])
            kp_l.append(kp)
            vp_l.append(vp)
            ks_l.append(kn.astype(cache_swa_k.dtype))
            vs_l.append(vn.astype(cache_swa_v.dtype))
        else:
            s0 = jnp.zeros((xp.shape[0], GLA_HEADS, GLA_DK, GLA_DV), jnp.float32)
            yp, sp = gla_mixer(hp, s0, gla_w_in[j], gla_w_a1[j], gla_w_a2[j], gla_b_a[j], gla_norm[j], gla_w_o[j])
            ys, sn = gla_mixer(hs, state_gla[j].astype(jnp.float32), gla_w_in[j], gla_w_a1[j], gla_w_a2[j],
                               gla_b_a[j], gla_norm[j], gla_w_o[j])
            sp_l.append(sp.astype(x_prompt.dtype))
            ss_l.append(sn.astype(state_gla.dtype))
        xp = xp + yp
        xs = xs + ys
        xp = xp + mlp(xp, norm_mlp[i], mlp_w_up[i], mlp_w_down[i])
        xs = xs + mlp(xs, norm_mlp[i], mlp_w_up[i], mlp_w_down[i])
    y_prompt = rms_norm(xp, norm_final)
    y_sample = rms_norm(xs, norm_final)
    new_swa_k_prompt = jnp.stack(kp_l)
    new_swa_v_prompt = jnp.stack(vp_l)
    new_gla_state_prompt = jnp.stack(sp_l)
    new_swa_k_sample = jnp.stack(ks_l)
    new_swa_v_sample = jnp.stack(vs_l)
    new_gla_state_sample = jnp.stack(ss_l)
    return (y_prompt, y_sample, new_swa_k_prompt, new_swa_v_prompt, new_gla_state_prompt,
            new_swa_k_sample, new_swa_v_sample, new_gla_state_sample)
```

```python
import functools

import jax
import jax.numpy as jnp
from jax import lax
from jax.experimental import pallas as pl
from jax.experimental.pallas import tpu as pltpu

F32 = jnp.float32
BF16 = jnp.bfloat16

D_MODEL = 2048
BATCH = 4
SEQ = 2048
DEPTH = 4
DEC_BATCH = 32
DEC_SEQ = 4
PAST_LEN = 16384
HEAD_DIM = 64
N_HEADS = D_MODEL // HEAD_DIM
N_KV = N_HEADS // 8
GROUP = N_HEADS // N_KV
WINDOW = 128
CACHE_W = WINDOW
ROPE_THETA = 10000.0
ATTN_SCALE = HEAD_DIM ** -0.5
GLA_HEADS = 4
GLA_DK = D_MODEL // 2 // GLA_HEADS
GLA_DV = D_MODEL // GLA_HEADS
GLA_RANK = 16
GLA_TAU = 16.0
GLA_CHUNK = 32
D_FF = 4 * D_MODEL
EPS = 1e-6

M_PROMPT = BATCH * SEQ
M_SAMPLE = DEC_BATCH * DEC_SEQ
M_ALL = M_PROMPT + M_SAMPLE

NQ = N_HEADS * HEAD_DIM
NKV = N_KV * HEAD_DIM
GLA_HK = GLA_HEADS * GLA_DK
GLA_HV = GLA_HEADS * GLA_DV

LANES = 128
SUBLANES = 8
VMEM_LIMIT = 56 * 1024 * 1024

BM = 640
NCH = 512
BF = 1024
DEC_PAD = SUBLANES
SAMPLE_GROUP = 8

assert M_ALL % BM == 0 and D_FF % BF == 0 and D_MODEL % NCH == 0
assert HEAD_DIM * 2 == LANES


def _params(sem, vmem=VMEM_LIMIT):
    return pltpu.CompilerParams(dimension_semantics=sem, vmem_limit_bytes=vmem)


def _resident(shape):
    return pl.BlockSpec(shape, lambda *_: (0,) * len(shape), pipeline_mode=pl.Buffered(1))


def _rms(x, g):
    return x * lax.rsqrt(jnp.mean(x * x, axis=-1, keepdims=True) + EPS) * g


def _log_sigmoid(z):
    return jnp.minimum(z, 0.0) - jnp.log1p(jnp.exp(-jnp.abs(z)))


def _dot(a, b):
    return jnp.dot(a, b, preferred_element_type=F32)


def _dot_nt(a, b):
    return lax.dot_general(a, b, (((1,), (1,)), ((), ())), preferred_element_type=F32)


def _dot_tn(a, b):
    return lax.dot_general(a, b, (((0,), (0,)), ((), ())), preferred_element_type=F32)


def _cumsum_rows(tri, x):
    return jnp.dot(tri, x, precision=lax.Precision.HIGHEST, preferred_element_type=F32)


def _swa_qkv_kernel(x_ref, g_ref, w_ref, cos_ref, sin_ref, q_ref, kv_ref):
    h = _rms(x_ref[...], g_ref[...]).astype(BF16)
    cos = cos_ref[...]
    sin = sin_ref[...]
    lane = lax.broadcasted_iota(jnp.int32, cos.shape, 1)
    first_half = (lane & (HEAD_DIM - 1)) < HEAD_DIM // 2

    def rope(a):
        partner = jnp.where(first_half,
                            pltpu.roll(a, LANES - HEAD_DIM // 2, 1),
                            pltpu.roll(a, HEAD_DIM // 2, 1))
        return a * cos + partner * sin

    for c in range(0, NQ, NCH):
        acc = _dot(h, w_ref[:, c:c + NCH])
        for j in range(0, NCH, LANES):
            q_ref[:, c + j:c + j + LANES] = rope(acc[:, j:j + LANES]).astype(BF16)
    acc = _dot(h, w_ref[:, NQ:NQ + 2 * NKV])
    for j in range(0, NKV, LANES):
        kv_ref[:, j:j + LANES] = rope(acc[:, j:j + LANES])
    kv_ref[:, NKV:] = acc[:, NKV:]


def _swa_qkv(x, g, w, cos, sin):
    return pl.pallas_call(
        _swa_qkv_kernel,
        out_shape=(jax.ShapeDtypeStruct((M_ALL, NQ), BF16),
                   jax.ShapeDtypeStruct((M_ALL, 2 * NKV), F32)),
        grid=(M_ALL // BM,),
        in_specs=[pl.BlockSpec((BM, D_MODEL), lambda i: (i, 0)),
                  _resident((1, D_MODEL)),
                  _resident((D_MODEL, NQ + 2 * NKV)),
                  pl.BlockSpec((BM, LANES), lambda i: (i, 0)),
                  pl.BlockSpec((BM, LANES), lambda i: (i, 0))],
        out_specs=(pl.BlockSpec((BM, NQ), lambda i: (i, 0)),
                   pl.BlockSpec((BM, 2 * NKV), lambda i: (i, 0))),
        compiler_params=_params(("arbitrary",)),
        name="swa_qkv",
    )(x, g, w, cos, sin)


def _sink_softmax(s, sk):
    m = jnp.maximum(jnp.max(s, axis=-1, keepdims=True), sk)
    p = jnp.exp(s - m)
    denom = jnp.sum(p, axis=-1, keepdims=True) + jnp.exp(sk - m)
    return p / denom


def _swa_prompt_kernel(sink_ref, q_ref, kvc_ref, kvp_ref, o_ref):
    n = pl.program_id(1)
    row = lax.broadcasted_iota(jnp.int32, (WINDOW, 2 * WINDOW), 0)
    col = lax.broadcasted_iota(jnp.int32, (WINDOW, 2 * WINDOW), 1)
    valid = (col > row) & (col <= row + WINDOW) & ((col >= WINDOW) | (n > 0))
    kvc = kvc_ref[...]
    kvp = kvp_ref[...]
    kcat = jnp.concatenate([kvp[:, :NKV], kvc[:, :NKV]], axis=0).astype(BF16)
    vcat = jnp.concatenate([kvp[:, NKV:], kvc[:, NKV:]], axis=0).astype(BF16)
    for kh in range(N_KV):
        k = kcat[:, kh * HEAD_DIM:(kh + 1) * HEAD_DIM]
        v = vcat[:, kh * HEAD_DIM:(kh + 1) * HEAD_DIM]
        for g in range(GROUP):
            h = kh * GROUP + g
            qh = q_ref[:, h * HEAD_DIM:(h + 1) * HEAD_DIM]
            s = _dot_nt(qh, k) * ATTN_SCALE
            s = jnp.where(valid, s, -jnp.inf)
            p = _sink_softmax(s, sink_ref[h])
            o_ref[:, h * HEAD_DIM:(h + 1) * HEAD_DIM] = _dot(p.astype(BF16), v).astype(BF16)


def _swa_prompt(sink, q, kv):
    nb = SEQ // WINDOW
    return pl.pallas_call(
        _swa_prompt_kernel,
        out_shape=jax.ShapeDtypeStruct((M_ALL, NQ), BF16),
        grid=(BATCH, nb),
        in_specs=[pl.BlockSpec(memory_space=pltpu.SMEM),
                  pl.BlockSpec((WINDOW, NQ), lambda b, n: (b * nb + n, 0)),
                  pl.BlockSpec((WINDOW, 2 * NKV), lambda b, n: (b * nb + n, 0)),
                  pl.BlockSpec((WINDOW, 2 * NKV), lambda b, n: (b * nb + jnp.maximum(n - 1, 0), 0))],
        out_specs=pl.BlockSpec((WINDOW, NQ), lambda b, n: (b * nb + n, 0)),
        compiler_params=_params(("arbitrary", "arbitrary")),
        name="swa_prompt_attn",
    )(sink, q, kv, kv)


def _swa_sample_kernel(sink_ref, q_ref, kvn_ref, ck_ref, cv_ref, o_ref):
    rows_q = SAMPLE_GROUP * DEC_SEQ
    n_rows = GROUP * rows_q
    n_cache = SAMPLE_GROUP * CACHE_W
    r = lax.broadcasted_iota(jnp.int32, (n_rows, 1), 0) & (rows_q - 1)
    bq = r >> 2
    tq = r & (DEC_SEQ - 1)
    cc = lax.broadcasted_iota(jnp.int32, (1, n_cache), 1)
    valid_c = ((cc >> 7) == bq) & ((cc & (CACHE_W - 1)) > tq)
    cn = lax.broadcasted_iota(jnp.int32, (1, rows_q), 1)
    valid_n = ((cn >> 2) == bq) & ((cn & (DEC_SEQ - 1)) <= tq)
    kvn = kvn_ref[...].astype(BF16)
    for kh in range(N_KV):
        ks = slice(kh * HEAD_DIM, (kh + 1) * HEAD_DIM)
        vs = slice(NKV + kh * HEAD_DIM, NKV + (kh + 1) * HEAD_DIM)
        ck = ck_ref[:, ks].astype(BF16)
        cv = cv_ref[:, ks].astype(BF16)
        q = jnp.concatenate(
            [q_ref[:, (kh * GROUP + g) * HEAD_DIM:(kh * GROUP + g + 1) * HEAD_DIM]
             for g in range(GROUP)], axis=0)
        sk = jnp.concatenate(
            [jnp.full((rows_q, 1), sink_ref[kh * GROUP + g], F32) for g in range(GROUP)], axis=0)
        s_c = jnp.where(valid_c, _dot_nt(q, ck) * ATTN_SCALE, -jnp.inf)
        s_n = jnp.where(valid_n, _dot_nt(q, kvn[:, ks]) * ATTN_SCALE, -jnp.inf)
        m = jnp.maximum(jnp.maximum(jnp.max(s_c, axis=-1, keepdims=True),
                                    jnp.max(s_n, axis=-1, keepdims=True)), sk)
        p_c = jnp.exp(s_c - m)
        p_n = jnp.exp(s_n - m)
        denom = (jnp.sum(p_c, axis=-1, keepdims=True) + jnp.sum(p_n, axis=-1, keepdims=True)
                 + jnp.exp(sk - m))
        o = _dot((p_c / denom).astype(BF16), cv) + _dot((p_n / denom).astype(BF16), kvn[:, vs])
        for g in range(GROUP):
            h = kh * GROUP + g
            o_ref[:, h * HEAD_DIM:(h + 1) * HEAD_DIM] = o[g * rows_q:(g + 1) * rows_q].astype(BF16)


def _swa_sample(sink, q_s, kv_s, cache_k, cache_v):
    rows_q = SAMPLE_GROUP * DEC_SEQ
    n_cache = SAMPLE_GROUP * CACHE_W
    return pl.pallas_call(
        _swa_sample_kernel,
        out_shape=jax.ShapeDtypeStruct((M_SAMPLE, NQ), BF16),
        grid=(DEC_BATCH // SAMPLE_GROUP,),
        in_specs=[pl.BlockSpec(memory_space=pltpu.SMEM),
                  pl.BlockSpec((rows_q, NQ), lambda i: (i, 0)),
                  pl.BlockSpec((rows_q, 2 * NKV), lambda i: (i, 0)),
                  pl.BlockSpec((n_cache, NKV), lambda i: (i, 0)),
                  pl.BlockSpec((n_cache, NKV), lambda i: (i, 0))],
        out_specs=pl.BlockSpec((rows_q, NQ), lambda i: (i, 0)),
        compiler_params=_params(("arbitrary",)),
        name="swa_sample_attn",
    )(sink, q_s, kv_s, cache_k, cache_v)


def _oproj_kernel(a_ref, w_ref, x_ref, o_ref):
    a = a_ref[...]
    for c in range(0, D_MODEL, NCH):
        o_ref[:, c:c + NCH] = x_ref[:, c:c + NCH] + _dot(a, w_ref[:, c:c + NCH])


def _oproj(a, w, x):
    return pl.pallas_call(
        _oproj_kernel,
        out_shape=jax.ShapeDtypeStruct((M_ALL, D_MODEL), F32),
        grid=(M_ALL // BM,),
        in_specs=[pl.BlockSpec((BM, D_MODEL), lambda i: (i, 0)),
                  _resident((D_MODEL, D_MODEL)),
                  pl.BlockSpec((BM, D_MODEL), lambda i: (i, 0))],
        out_specs=pl.BlockSpec((BM, D_MODEL), lambda i: (i, 0)),
        input_output_aliases={2: 0},
        compiler_params=_params(("arbitrary",)),
        name="out_proj",
    )(a, w, x)


def _mlp_kernel(x_ref, g_ref, wu_ref, wd_ref, o_ref, h_scr):
    f = pl.program_id(1)

    @pl.when(f == 0)
    def _():
        x = x_ref[...]
        h_scr[...] = _rms(x, g_ref[...]).astype(BF16)
        o_ref[...] = x

    a = jnp.square(jnp.maximum(_dot(h_scr[...], wu_ref[...]), 0.0)).astype(BF16)
    o_ref[...] += _dot(a, wd_ref[...])


def _mlp(x, g, w_up, w_down):
    return pl.pallas_call(
        _mlp_kernel,
        out_shape=jax.ShapeDtypeStruct((M_ALL, D_MODEL), F32),
        grid=(M_ALL // BM, D_FF // BF),
        in_specs=[pl.BlockSpec((BM, D_MODEL), lambda i, f: (i, 0)),
                  _resident((1, D_MODEL)),
                  pl.BlockSpec((D_MODEL, BF), lambda i, f: (0, f)),
                  pl.BlockSpec((BF, D_MODEL), lambda i, f: (f, 0))],
        out_specs=pl.BlockSpec((BM, D_MODEL), lambda i, f: (i, 0)),
        scratch_shapes=[pltpu.VMEM((BM, D_MODEL), BF16)],
        input_output_aliases={0: 0},
        compiler_params=_params(("arbitrary", "arbitrary")),
        name="mlp",
    )(x, g, w_up, w_down)


GLA_NIN = 2 * GLA_HK + 2 * GLA_HV
GLA_NPROJ = GLA_NIN + LANES
GLA_NVGT = 2 * GLA_HV + LANES


def _gla_proj_kernel(x_ref, g_ref, w_ref, qk_ref, vgt_ref):
    h = _rms(x_ref[...], g_ref[...]).astype(BF16)
    for c in range(0, 2 * GLA_HK, NCH):
        qk_ref[:, c:c + NCH] = _dot(h, w_ref[:, c:c + NCH])
    for c in range(0, 2 * GLA_HV, NCH):
        vgt_ref[:, c:c + NCH] = _dot(h, w_ref[:, 2 * GLA_HK + c:2 * GLA_HK + c + NCH]).astype(BF16)
    vgt_ref[:, 2 * GLA_HV:] = _dot(h, w_ref[:, GLA_NIN:]).astype(BF16)


def _gla_proj(x, g, w):
    bm = BM // 2
    return pl.pallas_call(
        _gla_proj_kernel,
        out_shape=(jax.ShapeDtypeStruct((M_ALL, 2 * GLA_HK), F32),
                   jax.ShapeDtypeStruct((M_ALL, GLA_NVGT), BF16)),
        grid=(M_ALL // bm,),
        in_specs=[pl.BlockSpec((bm, D_MODEL), lambda i: (i, 0)),
                  _resident((1, D_MODEL)),
                  _resident((D_MODEL, GLA_NPROJ))],
        out_specs=(pl.BlockSpec((bm, 2 * GLA_HK), lambda i: (i, 0)),
                   pl.BlockSpec((bm, GLA_NVGT), lambda i: (i, 0))),
        compiler_params=_params(("arbitrary",)),
        name="gla_proj",
    )(x, g, w)


def _gate(o, gn, g):
    on = o * lax.rsqrt(jnp.mean(o * o, axis=-1, keepdims=True) + EPS) * gn
    return on * (g * (1.0 / (1.0 + jnp.exp(-g))))


def _gla_prompt_kernel(q_ref, k_ref, v_ref, g_ref, t_ref, wa2_ref, ba_ref, gn_ref,
                       o_ref, st_ref, la_scr, s_scr):
    c = GLA_CHUNK
    z = _dot(t_ref[...], wa2_ref[...]) + ba_ref[...]
    la_scr[...] = _log_sigmoid(z) * (1.0 / GLA_TAU)
    s_scr[...] = jnp.zeros_like(s_scr)
    row = lax.broadcasted_iota(jnp.int32, (c, c), 0)
    col = lax.broadcasted_iota(jnp.int32, (c, c), 1)
    causal = row >= col
    tri = causal.astype(F32)
    gn = gn_ref[...]

    def chunk(n, carry):
        r0 = pl.multiple_of(n * c, c)
        rows = pl.ds(r0, c)
        cum = _cumsum_rows(tri, la_scr[rows, :])
        last = cum[c - 1:c, :]
        kc = k_ref[rows, :]
        vc = v_ref[rows, :]
        qe = (q_ref[rows, :] * (GLA_DK ** -0.5) * jnp.exp(cum)).astype(BF16)
        ke = (kc * jnp.exp(-cum)).astype(BF16)
        kd = (kc * jnp.exp(last - cum)).astype(BF16)
        att = jnp.where(causal, _dot_nt(qe, ke), 0.0).astype(BF16)
        s_t = s_scr[...]
        o = _dot(att, vc) + _dot_nt(qe, s_t.astype(BF16))
        s_scr[...] = s_t * jnp.exp(last) + _dot_tn(vc, kd)
        o_ref[rows, :] = _gate(o, gn, g_ref[rows, :].astype(F32)).astype(BF16)
        return carry

    lax.fori_loop(0, SEQ // c, chunk, 0)
    st_ref[0, 0] = s_scr[...].T


def _gla_prompt(qk, vgt, wa2, ba, gn):
    kb = GLA_HEADS
    gb = GLA_HEADS
    tb = 2 * GLA_HV // LANES
    return pl.pallas_call(
        _gla_prompt_kernel,
        out_shape=(jax.ShapeDtypeStruct((M_ALL, GLA_HV), BF16),
                   jax.ShapeDtypeStruct((BATCH, GLA_HEADS, GLA_DK, GLA_DV), F32)),
        grid=(BATCH, GLA_HEADS),
        in_specs=[pl.BlockSpec((SEQ, GLA_DK), lambda b, h: (b, h)),
                  pl.BlockSpec((SEQ, GLA_DK), lambda b, h: (b, kb + h)),
                  pl.BlockSpec((SEQ, GLA_DV), lambda b, h: (b, h)),
                  pl.BlockSpec((SEQ, GLA_DV), lambda b, h: (b, gb + h)),
                  pl.BlockSpec((SEQ, LANES), lambda b, h: (b, tb)),
                  pl.BlockSpec((LANES, GLA_DK), lambda b, h: (0, h)),
                  pl.BlockSpec((1, GLA_DK), lambda b, h: (0, h)),
                  _resident((1, GLA_DV))],
        out_specs=(pl.BlockSpec((SEQ, GLA_DV), lambda b, h: (b, h)),
                   pl.BlockSpec((1, 1, GLA_DK, GLA_DV), lambda b, h: (b, h, 0, 0))),
        scratch_shapes=[pltpu.VMEM((SEQ, GLA_DK), F32),
                        pltpu.VMEM((GLA_DV, GLA_DK), F32)],
        compiler_params=_params(("arbitrary", "arbitrary")),
        name="gla_prompt",
    )(qk, qk, vgt, vgt, vgt, wa2, ba, gn)


def _gla_sample_kernel(qk_ref, vgt_ref, s0_ref, wa2_ref, wa2t_ref, ba_ref, bat_ref, gn_ref,
                       o_ref, s1_ref):
    c = DEC_PAD
    row = lax.broadcasted_iota(jnp.int32, (c, c), 0)
    col = lax.broadcasted_iota(jnp.int32, (c, c), 1)
    causal = row >= col
    tri = causal.astype(F32)
    real_row = lax.broadcasted_iota(jnp.int32, (c, GLA_DK), 0) < DEC_SEQ
    real_col = lax.broadcasted_iota(jnp.int32, (GLA_DK, c), 1) < DEC_SEQ
    t = vgt_ref[0, :, 2 * GLA_HV:].astype(BF16)
    gn = gn_ref[...]
    for h in range(GLA_HEADS):
        ks = slice(h * GLA_DK, (h + 1) * GLA_DK)
        vs = slice(h * GLA_DV, (h + 1) * GLA_DV)
        z = _dot(t, wa2_ref[:, ks]) + ba_ref[:, ks]
        la = jnp.where(real_row, _log_sigmoid(z) * (1.0 / GLA_TAU), 0.0)
        cum = _cumsum_rows(tri, la)
        last = cum[c - 1:c, :]
        z_t = _dot_nt(wa2t_ref[ks, :], t) + bat_ref[ks, :]
        la_t = jnp.where(real_col, _log_sigmoid(z_t) * (1.0 / GLA_TAU), 0.0)
        last_t = jnp.sum(la_t, axis=1, keepdims=True)
        kc = qk_ref[0, :, GLA_HK + h * GLA_DK:GLA_HK + (h + 1) * GLA_DK]
        vc = vgt_ref[0, :, vs].astype(BF16)
        qe = (qk_ref[0, :, ks] * (GLA_DK ** -0.5) * jnp.exp(cum)).astype(BF16)
        ke = (kc * jnp.exp(-cum)).astype(BF16)
        kd = (kc * jnp.exp(last - cum)).astype(BF16)
        att = jnp.where(causal, _dot_nt(qe, ke), 0.0).astype(BF16)
        s0 = s0_ref[0, h]
        o = _dot(att, vc) + _dot(qe, s0.astype(BF16))
        s1_ref[0, h] = s0 * jnp.exp(last_t) + _dot_tn(kd, vc)
        o_ref[0, :, vs] = _gate(o, gn, vgt_ref[0, :, GLA_HV + h * GLA_DV:GLA_HV + (h + 1) * GLA_DV])


def _gla_sample(qk_s, vgt_s, s0, wa2, wa2t, ba, bat, gn):
    return pl.pallas_call(
        _gla_sample_kernel,
        out_shape=(jax.ShapeDtypeStruct((DEC_BATCH, DEC_PAD, GLA_HV), F32),
                   jax.ShapeDtypeStruct((DEC_BATCH, GLA_HEADS, GLA_DK, GLA_DV), F32)),
        grid=(DEC_BATCH,),
        in_specs=[pl.BlockSpec((1, DEC_PAD, 2 * GLA_HK), lambda b: (b, 0, 0)),
                  pl.BlockSpec((1, DEC_PAD, GLA_NVGT), lambda b: (b, 0, 0)),
                  pl.BlockSpec((1, GLA_HEADS, GLA_DK, GLA_DV), lambda b: (b, 0, 0, 0)),
                  _resident((LANES, GLA_HK)),
                  _resident((GLA_HK, LANES)),
                  _resident((1, GLA_HK)),
                  _resident((GLA_HK, 1)),
                  _resident((1, GLA_DV))],
        out_specs=(pl.BlockSpec((1, DEC_PAD, GLA_HV), lambda b: (b, 0, 0)),
                   pl.BlockSpec((1, GLA_HEADS, GLA_DK, GLA_DV), lambda b: (b, 0, 0, 0))),
        compiler_params=_params(("arbitrary",)),
        name="gla_sample",
    )(qk_s, vgt_s, s0, wa2, wa2t, ba, bat, gn)


def _final_norm_kernel(x_ref, g_ref, o_ref):
    o_ref[...] = _rms(x_ref[...], g_ref[...])


def _final_norm(x, g):
    return pl.pallas_call(
        _final_norm_kernel,
        out_shape=jax.ShapeDtypeStruct((M_ALL, D_MODEL), F32),
        grid=(M_ALL // BM,),
        in_specs=[pl.BlockSpec((BM, D_MODEL), lambda i: (i, 0)), _resident((1, D_MODEL))],
        out_specs=pl.BlockSpec((BM, D_MODEL), lambda i: (i, 0)),
        compiler_params=_params(("arbitrary",)),
        name="final_norm",
    )(x, g)


def _rope_tables():
    half = HEAD_DIM // 2
    inv = ROPE_THETA ** (-jnp.arange(half, dtype=F32) * 2.0 / HEAD_DIM)
    pos = jnp.concatenate([
        jnp.tile(jnp.arange(SEQ, dtype=F32), BATCH),
        jnp.tile((PAST_LEN + jnp.arange(DEC_SEQ)).astype(F32), DEC_BATCH)])
    ang = pos[:, None] * inv[None, :]
    cos = jnp.tile(jnp.cos(ang), (1, LANES // half))
    sin = jnp.sin(ang)
    sin = jnp.tile(jnp.concatenate([-sin, sin], axis=1), (1, LANES // HEAD_DIM))
    return cos, sin


def _pad_sample(a):
    a = a.reshape(DEC_BATCH, DEC_SEQ, a.shape[-1])
    return jnp.pad(a, ((0, 0), (0, DEC_PAD - DEC_SEQ), (0, 0)))


def kernel(x_prompt, x_sample, cache_swa_k, cache_swa_v, state_gla, norm_attn, norm_mlp, norm_final, swa_w_qkv, swa_sink, swa_w_o, gla_w_in, gla_w_a1, gla_w_a2, gla_b_a, gla_norm, gla_w_o, mlp_w_up, mlp_w_down):
    x = jnp.concatenate([x_prompt.reshape(M_PROMPT, D_MODEL), x_sample.reshape(M_SAMPLE, D_MODEL)])
    cos, sin = _rope_tables()
    kp_l, vp_l, sp_l, ks_l, vs_l, ss_l = [], [], [], [], [], []
    for i in range(DEPTH):
        j = i // 2
        g_attn = norm_attn[i].reshape(1, D_MODEL)
        if i % 2 == 0:
            q, kv = _swa_qkv(x, g_attn, swa_w_qkv[j].astype(BF16), cos, sin)
            ck = cache_swa_k[j].reshape(DEC_BATCH * CACHE_W, NKV)
            cv = cache_swa_v[j].reshape(DEC_BATCH * CACHE_W, NKV)
            o = _swa_prompt(swa_sink[j], q, kv)
            o_s = _swa_sample(swa_sink[j], q[M_PROMPT:], kv[M_PROMPT:], ck, cv)
            o = lax.dynamic_update_slice(o, o_s, (M_PROMPT, 0))
            x = _oproj(o, swa_w_o[j].astype(BF16), x)
            kv_p = kv[:M_PROMPT].reshape(BATCH, SEQ, 2, N_KV, HEAD_DIM)[:, SEQ - CACHE_W:]
            kp_l.append(kv_p[:, :, 0])
            vp_l.append(kv_p[:, :, 1])
            kv_s = kv[M_PROMPT:].reshape(DEC_BATCH, DEC_SEQ, 2, N_KV, HEAD_DIM)
            ks_l.append(jnp.concatenate([cache_swa_k[j][:, DEC_SEQ:], kv_s[:, :, 0]], axis=1))
            vs_l.append(jnp.concatenate([cache_swa_v[j][:, DEC_SEQ:], kv_s[:, :, 1]], axis=1))
        else:
            w_a1 = jnp.pad(gla_w_a1[j], ((0, 0), (0, LANES - GLA_RANK)))
            w = jnp.concatenate([gla_w_in[j], w_a1], axis=1).astype(BF16)
            wa2 = jnp.pad(gla_w_a2[j], ((0, LANES - GLA_RANK), (0, 0))).astype(BF16)
            ba = gla_b_a[j].reshape(1, GLA_HK)
            gn = gla_norm[j].reshape(1, GLA_DV)
            qk, vgt = _gla_proj(x, g_attn, w)
            o, sp = _gla_prompt(qk, vgt, wa2, ba, gn)
            o_s, ss = _gla_sample(_pad_sample(qk[M_PROMPT:]), _pad_sample(vgt[M_PROMPT:].astype(F32)),
                                  state_gla[j], wa2, wa2.T, ba, ba.reshape(GLA_HK, 1), gn)
            o_s = o_s[:, :DEC_SEQ].reshape(M_SAMPLE, GLA_HV).astype(BF16)
            o = lax.dynamic_update_slice(o, o_s, (M_PROMPT, 0))
            x = _oproj(o, gla_w_o[j].astype(BF16), x)
            sp_l.append(sp)
            ss_l.append(ss)
        x = _mlp(x, norm_mlp[i].reshape(1, D_MODEL), mlp_w_up[i].astype(BF16), mlp_w_down[i].astype(BF16))
    y = _final_norm(x, norm_final.reshape(1, D_MODEL))
    return (y[:M_PROMPT].reshape(BATCH, SEQ, D_MODEL),
            y[M_PROMPT:].reshape(DEC_BATCH, DEC_SEQ, D_MODEL),
            jnp.stack(kp_l), jnp.stack(vp_l), jnp.stack(sp_l),
            jnp.stack(ks_l), jnp.stack(vs_l), jnp.stack(ss_l))
```

```python
import jax
import jax.numpy as jnp
from jax import lax
from jax.experimental import pallas as pl
from jax.experimental.pallas import tpu as pltpu

F32 = jnp.float32
BF16 = jnp.bfloat16

D_MODEL = 2048
BATCH = 4
SEQ = 2048
DEPTH = 4
DEC_BATCH = 32
DEC_SEQ = 4
PAST_LEN = 16384
HEAD_DIM = 64
N_HEADS = D_MODEL // HEAD_DIM
N_KV = N_HEADS // 8
GROUP = N_HEADS // N_KV
WINDOW = 128
CACHE_W = WINDOW
ROPE_THETA = 10000.0
ATTN_SCALE = HEAD_DIM ** -0.5
GLA_HEADS = 4
GLA_DK = D_MODEL // 2 // GLA_HEADS
GLA_DV = D_MODEL // GLA_HEADS
GLA_RANK = 16
GLA_TAU = 16.0
GLA_CHUNK = 32
D_FF = 4 * D_MODEL
EPS = 1e-6
N_SWA = (DEPTH + 1) // 2
N_GLA = DEPTH // 2

M_PROMPT = BATCH * SEQ
M_SAMPLE = DEC_BATCH * DEC_SEQ
M_ALL = M_PROMPT + M_SAMPLE

NQ = N_HEADS * HEAD_DIM
NKV = N_KV * HEAD_DIM
GLA_HK = GLA_HEADS * GLA_DK
GLA_HV = GLA_HEADS * GLA_DV
GLA_NIN = 2 * GLA_HK + 2 * GLA_HV
GLA_NVGT = 2 * GLA_HV + 128

LANES = 128
SUBLANES = 8
VMEM_LIMIT = 56 * 1024 * 1024

BM = 640
NCH = 512
BF = 1024
DEC_PAD = SUBLANES
SAMPLE_GROUP = 8
GLA_SUPER = 128

assert M_ALL % BM == 0 and D_FF % BF == 0 and D_MODEL % NCH == 0
assert HEAD_DIM * 2 == LANES and GLA_NVGT == 2 * GLA_HV + LANES
assert GLA_SUPER % GLA_CHUNK == 0 and SEQ % GLA_SUPER == 0


def _params(sem, vmem=VMEM_LIMIT):
    return pltpu.CompilerParams(dimension_semantics=sem, vmem_limit_bytes=vmem)


def _resident(shape):
    return pl.BlockSpec(shape, lambda *_: (0,) * len(shape), pipeline_mode=pl.Buffered(1))


def _layer(shape, j):
    return pl.BlockSpec((None,) + tuple(shape), lambda *_: (j,) + (0,) * len(shape),
                        pipeline_mode=pl.Buffered(1))


def _rms(x, g):
    return x * lax.rsqrt(jnp.mean(x * x, axis=-1, keepdims=True) + EPS) * g


def _log_sigmoid(z):
    return jnp.minimum(z, 0.0) - jnp.log1p(jnp.exp(-jnp.abs(z)))


def _dot(a, b):
    return jnp.dot(a, b, preferred_element_type=F32)


def _dot_nt(a, b):
    return lax.dot_general(a, b, (((1,), (1,)), ((), ())), preferred_element_type=F32)


def _dot_tn(a, b):
    return lax.dot_general(a, b, (((0,), (0,)), ((), ())), preferred_element_type=F32)


def _dot_exact_lhs(t, x):
    hi = x.astype(BF16)
    r1 = x - hi.astype(F32)
    mid = r1.astype(BF16)
    lo = (r1 - mid.astype(F32)).astype(BF16)
    return _dot(t, hi) + _dot(t, mid) + _dot(t, lo)


def _swa_qkv_kernel(x_ref, g_ref, w_ref, cos_ref, sin_ref, q_ref, kv_ref):
    h = _rms(x_ref[...], g_ref[...]).astype(BF16)
    cos = cos_ref[...]
    sin = sin_ref[...]
    lane = lax.broadcasted_iota(jnp.int32, cos.shape, 1)
    first_half = (lane & (HEAD_DIM - 1)) < HEAD_DIM // 2

    def rope(a):
        partner = jnp.where(first_half,
                            pltpu.roll(a, LANES - HEAD_DIM // 2, 1),
                            pltpu.roll(a, HEAD_DIM // 2, 1))
        return a * cos + partner * sin

    for c in range(0, NQ, NCH):
        acc = _dot(h, w_ref[:, c:c + NCH])
        for j in range(0, NCH, LANES):
            q_ref[:, c + j:c + j + LANES] = (rope(acc[:, j:j + LANES]) * ATTN_SCALE).astype(BF16)
    acc = _dot(h, w_ref[:, NQ:NQ + 2 * NKV])
    for j in range(0, NKV, LANES):
        kv_ref[:, j:j + LANES] = rope(acc[:, j:j + LANES])
    kv_ref[:, NKV:] = acc[:, NKV:]


def _swa_qkv(x, g, w_all, j, cos, sin):
    return pl.pallas_call(
        _swa_qkv_kernel,
        out_shape=(jax.ShapeDtypeStruct((M_ALL, NQ), BF16),
                   jax.ShapeDtypeStruct((M_ALL, 2 * NKV), F32)),
        grid=(M_ALL // BM,),
        in_specs=[pl.BlockSpec((BM, D_MODEL), lambda i: (i, 0)),
                  _resident((1, D_MODEL)),
                  _layer((D_MODEL, NQ + 2 * NKV), j),
                  pl.BlockSpec((BM, LANES), lambda i: (i, 0)),
                  pl.BlockSpec((BM, LANES), lambda i: (i, 0))],
        out_specs=(pl.BlockSpec((BM, NQ), lambda i: (i, 0)),
                   pl.BlockSpec((BM, 2 * NKV), lambda i: (i, 0))),
        compiler_params=_params(("arbitrary",)),
        name="swa_qkv",
    )(x, g, w_all, cos, sin)


def _half_lane_pair(tile, in_high):
    lane = lax.broadcasted_iota(jnp.int32, tile.shape, 1)
    other = pltpu.roll(tile, HEAD_DIM, 1)
    zero = jnp.zeros_like(tile)
    low = jnp.where(lane < HEAD_DIM, other if in_high else tile, zero)
    high = jnp.where(lane >= HEAD_DIM, tile if in_high else other, zero)
    return low, high


def _swa_prompt_kernel(sink_ref, q_ref, kvc_ref, kvp_ref, o_ref):
    n = pl.program_id(1)
    pairs = GROUP // 2
    rows = pairs * WINDOW
    row = lax.broadcasted_iota(jnp.int32, (rows, 2 * WINDOW), 0) & (WINDOW - 1)
    col = lax.broadcasted_iota(jnp.int32, (rows, 2 * WINDOW), 1)
    valid = (col > row) & (col <= row + WINDOW) & ((col >= WINDOW) | (n > 0))
    kvcat = jnp.concatenate([kvp_ref[...], kvc_ref[...]], axis=0)
    for kh in range(N_KV):
        t = kh // 2
        k_lo, k_hi = _half_lane_pair(kvcat[:, t * LANES:(t + 1) * LANES], kh % 2 == 1)
        v_lo, v_hi = _half_lane_pair(kvcat[:, NKV + t * LANES:NKV + (t + 1) * LANES], kh % 2 == 1)
        q = jnp.concatenate([q_ref[:, (kh * pairs + p) * LANES:(kh * pairs + p + 1) * LANES]
                             for p in range(pairs)], axis=0)
        acc = [None] * pairs
        for half, (k, v) in enumerate(((k_lo, v_lo), (k_hi, v_hi))):
            s = jnp.where(valid, _dot_nt(q, k.astype(BF16)), -jnp.inf)
            es, rds = [], []
            for p in range(pairs):
                sp = s[p * WINDOW:(p + 1) * WINDOW]
                sk = sink_ref[kh * GROUP + 2 * p + half]
                m = jnp.max(jnp.maximum(sp, sk), axis=-1, keepdims=True)
                e = jnp.exp(sp - m)
                es.append(e.astype(BF16))
                rds.append(1.0 / (jnp.sum(e, axis=-1, keepdims=True) + jnp.exp(sk - m)))
            pv = _dot(jnp.concatenate(es, axis=0), v.astype(BF16))
            for p in range(pairs):
                part = pv[p * WINDOW:(p + 1) * WINDOW] * rds[p]
                acc[p] = part if acc[p] is None else acc[p] + part
        for p in range(pairs):
            c = (kh * pairs + p) * LANES
            o_ref[:, c:c + LANES] = acc[p].astype(BF16)


def _swa_prompt(sink_all, j, q, kv):
    nb = SEQ // WINDOW
    return pl.pallas_call(
        _swa_prompt_kernel,
        out_shape=jax.ShapeDtypeStruct((M_ALL, NQ), BF16),
        grid=(BATCH, nb),
        in_specs=[pl.BlockSpec(memory_space=pltpu.SMEM),
                  pl.BlockSpec((WINDOW, NQ), lambda b, n: (b * nb + n, 0)),
                  pl.BlockSpec((WINDOW, 2 * NKV), lambda b, n: (b * nb + n, 0)),
                  pl.BlockSpec((WINDOW, 2 * NKV), lambda b, n: (b * nb + jnp.maximum(n - 1, 0), 0))],
        out_specs=pl.BlockSpec((WINDOW, NQ), lambda b, n: (b * nb + n, 0)),
        compiler_params=_params(("arbitrary", "arbitrary")),
        name="swa_prompt_attn",
    )(sink_all[j], q, kv, kv)


def _swa_sample_kernel(sink_ref, q_ref, kvn_ref, ck_ref, cv_ref, o_ref):
    rows_q = SAMPLE_GROUP * DEC_SEQ
    n_rows = GROUP * rows_q
    n_cache = SAMPLE_GROUP * CACHE_W
    r = lax.broadcasted_iota(jnp.int32, (n_rows, 1), 0) & (rows_q - 1)
    bq = r >> 2
    tq = r & (DEC_SEQ - 1)
    cc = lax.broadcasted_iota(jnp.int32, (1, n_cache), 1)
    valid_c = ((cc >> 7) == bq) & ((cc & (CACHE_W - 1)) > tq)
    cn = lax.broadcasted_iota(jnp.int32, (1, rows_q), 1)
    valid_n = ((cn >> 2) == bq) & ((cn & (DEC_SEQ - 1)) <= tq)
    kvn = kvn_ref[...].astype(BF16)
    for kh in range(N_KV):
        ks = slice(kh * HEAD_DIM, (kh + 1) * HEAD_DIM)
        vs = slice(NKV + kh * HEAD_DIM, NKV + (kh + 1) * HEAD_DIM)
        ck = ck_ref[:, ks].astype(BF16)
        cv = cv_ref[:, ks].astype(BF16)
        q = jnp.concatenate(
            [q_ref[:, (kh * GROUP + g) * HEAD_DIM:(kh * GROUP + g + 1) * HEAD_DIM]
             for g in range(GROUP)], axis=0)
        sk = jnp.concatenate(
            [jnp.full((rows_q, 1), sink_ref[kh * GROUP + g], F32) for g in range(GROUP)], axis=0)
        s_c = jnp.where(valid_c, _dot_nt(q, ck), -jnp.inf)
        s_n = jnp.where(valid_n, _dot_nt(q, kvn[:, ks]), -jnp.inf)
        m = jnp.maximum(jnp.maximum(jnp.max(s_c, axis=-1, keepdims=True),
                                    jnp.max(s_n, axis=-1, keepdims=True)), sk)
        p_c = jnp.exp(s_c - m)
        p_n = jnp.exp(s_n - m)
        denom = (jnp.sum(p_c, axis=-1, keepdims=True) + jnp.sum(p_n, axis=-1, keepdims=True)
                 + jnp.exp(sk - m))
        o = _dot((p_c / denom).astype(BF16), cv) + _dot((p_n / denom).astype(BF16), kvn[:, vs])
        for g in range(GROUP):
            h = kh * GROUP + g
            o_ref[:, h * HEAD_DIM:(h + 1) * HEAD_DIM] = o[g * rows_q:(g + 1) * rows_q].astype(BF16)


def _swa_sample(sink_all, j, q, kv, cache_k, cache_v):
    rows_q = SAMPLE_GROUP * DEC_SEQ
    n_cache = SAMPLE_GROUP * CACHE_W
    first = M_PROMPT // rows_q
    return pl.pallas_call(
        _swa_sample_kernel,
        out_shape=jax.ShapeDtypeStruct((M_SAMPLE, NQ), BF16),
        grid=(DEC_BATCH // SAMPLE_GROUP,),
        in_specs=[pl.BlockSpec(memory_space=pltpu.SMEM),
                  pl.BlockSpec((rows_q, NQ), lambda i: (first + i, 0)),
                  pl.BlockSpec((rows_q, 2 * NKV), lambda i: (first + i, 0)),
                  pl.BlockSpec((None, n_cache, NKV), lambda i: (j, i, 0)),
                  pl.BlockSpec((None, n_cache, NKV), lambda i: (j, i, 0))],
        out_specs=pl.BlockSpec((rows_q, NQ), lambda i: (i, 0)),
        compiler_params=_params(("arbitrary",)),
        name="swa_sample_attn",
    )(sink_all[j], q, kv, cache_k, cache_v)


def _oproj_kernel(a_ref, w_ref, x_ref, o_ref):
    a = a_ref[...]
    for c in range(0, D_MODEL, NCH):
        o_ref[:, c:c + NCH] = x_ref[:, c:c + NCH] + _dot(a, w_ref[:, c:c + NCH])


def _oproj(a, w_all, j, x):
    return pl.pallas_call(
        _oproj_kernel,
        out_shape=jax.ShapeDtypeStruct((M_ALL, D_MODEL), F32),
        grid=(M_ALL // BM,),
        in_specs=[pl.BlockSpec((BM, D_MODEL), lambda i: (i, 0)),
                  _layer((D_MODEL, D_MODEL), j),
                  pl.BlockSpec((BM, D_MODEL), lambda i: (i, 0))],
        out_specs=pl.BlockSpec((BM, D_MODEL), lambda i: (i, 0)),
        input_output_aliases={2: 0},
        compiler_params=_params(("arbitrary",)),
        name="out_proj",
    )(a, w_all, x)


def _mlp_kernel(x_ref, g_ref, wu_ref, wd_ref, o_ref, h_scr):
    f = pl.program_id(1)

    @pl.when(f == 0)
    def _():
        x = x_ref[...]
        h_scr[...] = _rms(x, g_ref[...]).astype(BF16)
        o_ref[...] = x

    a = jnp.square(jnp.maximum(_dot(h_scr[...], wu_ref[...]), 0.0)).astype(BF16)
    o_ref[...] += _dot(a, wd_ref[...])


def _mlp(x, g, w_up_all, w_down_all, layer):
    return pl.pallas_call(
        _mlp_kernel,
        out_shape=jax.ShapeDtypeStruct((M_ALL, D_MODEL), F32),
        grid=(M_ALL // BM, D_FF // BF),
        in_specs=[pl.BlockSpec((BM, D_MODEL), lambda i, f: (i, 0)),
                  _resident((1, D_MODEL)),
                  pl.BlockSpec((None, D_MODEL, BF), lambda i, f: (layer, 0, f)),
                  pl.BlockSpec((None, BF, D_MODEL), lambda i, f: (layer, f, 0))],
        out_specs=pl.BlockSpec((BM, D_MODEL), lambda i, f: (i, 0)),
        scratch_shapes=[pltpu.VMEM((BM, D_MODEL), BF16)],
        input_output_aliases={0: 0},
        compiler_params=_params(("arbitrary", "arbitrary")),
        name="mlp",
    )(x, g, w_up_all, w_down_all)


def _gla_proj_kernel(x_ref, g_ref, w_ref, wa1_ref, qk_ref, vgt_ref):
    h = _rms(x_ref[...], g_ref[...]).astype(BF16)
    for c in range(0, 2 * GLA_HK, NCH):
        qk_ref[:, c:c + NCH] = _dot(h, w_ref[:, c:c + NCH])
    for c in range(0, 2 * GLA_HV, NCH):
        vgt_ref[:, c:c + NCH] = _dot(h, w_ref[:, 2 * GLA_HK + c:2 * GLA_HK + c + NCH]).astype(BF16)
    vgt_ref[:, 2 * GLA_HV:] = _dot(h, wa1_ref[...]).astype(BF16)


def _gla_proj(x, g, w_all, wa1_all, j):
    bm = BM // 2
    return pl.pallas_call(
        _gla_proj_kernel,
        out_shape=(jax.ShapeDtypeStruct((M_ALL, 2 * GLA_HK), F32),
                   jax.ShapeDtypeStruct((M_ALL, GLA_NVGT), BF16)),
        grid=(M_ALL // bm,),
        in_specs=[pl.BlockSpec((bm, D_MODEL), lambda i: (i, 0)),
                  _resident((1, D_MODEL)),
                  _layer((D_MODEL, GLA_NIN), j),
                  _layer((D_MODEL, LANES), j)],
        out_specs=(pl.BlockSpec((bm, 2 * GLA_HK), lambda i: (i, 0)),
                   pl.BlockSpec((bm, GLA_NVGT), lambda i: (i, 0))),
        compiler_params=_params(("arbitrary",)),
        name="gla_proj",
    )(x, g, w_all, wa1_all)


def _gate(o, gn, g):
    on = o * lax.rsqrt(jnp.mean(o * o, axis=-1, keepdims=True) + EPS) * gn
    return on * (g * (1.0 / (1.0 + jnp.exp(-g))))


def _gla_prompt_kernel(q_ref, k_ref, v_ref, g_ref, t_ref, wa2_ref, ba_ref, gn_ref, *rest):
    o_ref, st_ref, la_scr, s_scr = rest[-4:]
    c = GLA_CHUNK
    sc = GLA_SUPER
    nsub = sc // c
    z = _dot(t_ref[...], wa2_ref[...]) + ba_ref[...]
    la_scr[...] = _log_sigmoid(z) * (1.0 / GLA_TAU)
    s_scr[...] = jnp.zeros_like(s_scr)
    row = lax.broadcasted_iota(jnp.int32, (c, sc), 0)
    col = lax.broadcasted_iota(jnp.int32, (c, sc), 1)
    tr = lax.broadcasted_iota(jnp.int32, (sc, sc), 0)
    tc = lax.broadcasted_iota(jnp.int32, (sc, sc), 1)
    tri = (tc <= tr).astype(BF16)
    gn = gn_ref[...]

    def block(n, carry):
        r0 = pl.multiple_of(n * sc, sc)
        rows = pl.ds(r0, sc)
        gcum = _dot_exact_lhs(tri, la_scr[rows, :])
        gtot = gcum[sc - 1:sc]
        qs = q_ref[rows, :] * (GLA_DK ** -0.5)
        k = k_ref[rows, :]
        v = v_ref[rows, :]
        s_t = s_scr[...]
        o = _dot_nt((qs * jnp.exp(gcum)).astype(BF16), s_t.astype(BF16))
        slabs = []
        for i in range(nsub):
            lo, hi = i * c, (i + 1) * c
            piv = gcum[lo - 1:lo] if i > 0 else jnp.zeros((1, GLA_DK), F32)
            qe = (qs[lo:hi] * jnp.exp(gcum[lo:hi] - piv)).astype(BF16)
            ke = (k[:hi] * jnp.exp(piv - gcum[:hi])).astype(BF16)
            if hi < sc:
                ke = jnp.concatenate([ke, jnp.zeros((sc - hi, GLA_DK), BF16)], axis=0)
            slabs.append(jnp.where(col <= row + lo, _dot_nt(qe, ke), 0.0))
        att = jnp.concatenate(slabs, axis=0).astype(BF16)
        o = o + _dot(att, v)
        s_scr[...] = s_t * jnp.exp(gtot) + _dot_tn(v, (k * jnp.exp(gtot - gcum)).astype(BF16))
        o_ref[rows, :] = _gate(o, gn, g_ref[rows, :].astype(F32)).astype(BF16)
        return carry

    lax.fori_loop(0, SEQ // sc, block, 0, unroll=4)
    st_ref[0, 0] = s_scr[...].T


def _stacked_alias(prev, n_inputs, out_index):
    if prev is None:
        return [], [], {}
    return [pl.BlockSpec(memory_space=pl.ANY)], [prev], {n_inputs: out_index}


def _gla_prompt(qk, vgt, wa2_all, ba_all, gn_all, j, prev_state):
    kb = GLA_HEADS
    gb = GLA_HEADS
    tb = 2 * GLA_HV // LANES
    prev_spec, prev_arg, alias = _stacked_alias(prev_state, 8, 1)
    return pl.pallas_call(
        _gla_prompt_kernel,
        out_shape=(jax.ShapeDtypeStruct((M_ALL, GLA_HV), BF16),
                   jax.ShapeDtypeStruct((N_GLA, BATCH, GLA_HEADS, GLA_DK, GLA_DV), F32)),
        grid=(BATCH, GLA_HEADS),
        in_specs=[pl.BlockSpec((SEQ, GLA_DK), lambda b, h: (b, h)),
                  pl.BlockSpec((SEQ, GLA_DK), lambda b, h: (b, kb + h)),
                  pl.BlockSpec((SEQ, GLA_DV), lambda b, h: (b, h)),
                  pl.BlockSpec((SEQ, GLA_DV), lambda b, h: (b, gb + h)),
                  pl.BlockSpec((SEQ, LANES), lambda b, h: (b, tb)),
                  pl.BlockSpec((None, LANES, GLA_DK), lambda b, h: (j, 0, h)),
                  pl.BlockSpec((None, 1, GLA_DK), lambda b, h: (j, 0, h)),
                  _layer((1, GLA_DV), j)] + prev_spec,
        out_specs=(pl.BlockSpec((SEQ, GLA_DV), lambda b, h: (b, h)),
                   pl.BlockSpec((None, 1, 1, GLA_DK, GLA_DV), lambda b, h: (j, b, h, 0, 0))),
        scratch_shapes=[pltpu.VMEM((SEQ, GLA_DK), F32),
                        pltpu.VMEM((GLA_DV, GLA_DK), F32)],
        input_output_aliases=alias,
        compiler_params=_params(("arbitrary", "arbitrary")),
        name="gla_prompt",
    )(qk, qk, vgt, vgt, vgt, wa2_all, ba_all, gn_all, *prev_arg)


def _gla_sample_kernel(qk_ref, vgt_ref, s0_ref, wa2_ref, wa2t_ref, ba_ref, bat_ref, gn_ref, *rest):
    o_ref, s1_ref = rest[-2:]
    c = DEC_PAD
    row = lax.broadcasted_iota(jnp.int32, (c, c), 0)
    col = lax.broadcasted_iota(jnp.int32, (c, c), 1)
    causal = row >= col
    tri = causal.astype(BF16)
    real_row = lax.broadcasted_iota(jnp.int32, (c, GLA_DK), 0) < DEC_SEQ
    real_col = lax.broadcasted_iota(jnp.int32, (GLA_DK, c), 1) < DEC_SEQ
    t = vgt_ref[0, :, 2 * GLA_HV:].astype(BF16)
    gn = gn_ref[...]
    for h in range(GLA_HEADS):
        ks = slice(h * GLA_DK, (h + 1) * GLA_DK)
        vs = slice(h * GLA_DV, (h + 1) * GLA_DV)
        z = _dot(t, wa2_ref[:, ks]) + ba_ref[:, ks]
        la = jnp.where(real_row, _log_sigmoid(z) * (1.0 / GLA_TAU), 0.0)
        cum = _dot_exact_lhs(tri, la)
        last = cum[c - 1:c, :]
        z_t = _dot_nt(wa2t_ref[ks, :], t) + bat_ref[ks, :]
        la_t = jnp.where(real_col, _log_sigmoid(z_t) * (1.0 / GLA_TAU), 0.0)
        last_t = jnp.sum(la_t, axis=1, keepdims=True)
        kc = qk_ref[0, :, GLA_HK + h * GLA_DK:GLA_HK + (h + 1) * GLA_DK]
        vc = vgt_ref[0, :, vs].astype(BF16)
        qe = (qk_ref[0, :, ks] * (GLA_DK ** -0.5) * jnp.exp(cum)).astype(BF16)
        ke = (kc * jnp.exp(-cum)).astype(BF16)
        kd = (kc * jnp.exp(last - cum)).astype(BF16)
        att = jnp.where(causal, _dot_nt(qe, ke), 0.0).astype(BF16)
        s0 = s0_ref[0, h]
        o = _dot(att, vc) + _dot(qe, s0.astype(BF16))
        s1_ref[0, h] = s0 * jnp.exp(last_t) + _dot_tn(kd, vc)
        o_ref[0, :, vs] = _gate(o, gn, vgt_ref[0, :, GLA_HV + h * GLA_DV:GLA_HV + (h + 1) * GLA_DV])


def _gla_sample(qk_s, vgt_s, state_all, wa2_all, wa2t_all, ba_all, bat_all, gn_all, j, prev_state):
    state_block = (None, 1, GLA_HEADS, GLA_DK, GLA_DV)
    prev_spec, prev_arg, alias = _stacked_alias(prev_state, 8, 1)
    return pl.pallas_call(
        _gla_sample_kernel,
        out_shape=(jax.ShapeDtypeStruct((DEC_BATCH, DEC_PAD, GLA_HV), F32),
                   jax.ShapeDtypeStruct((N_GLA, DEC_BATCH, GLA_HEADS, GLA_DK, GLA_DV), F32)),
        grid=(DEC_BATCH,),
        in_specs=[pl.BlockSpec((1, DEC_PAD, 2 * GLA_HK), lambda b: (b, 0, 0)),
                  pl.BlockSpec((1, DEC_PAD, GLA_NVGT), lambda b: (b, 0, 0)),
                  pl.BlockSpec(state_block, lambda b: (j, b, 0, 0, 0)),
                  _layer((LANES, GLA_HK), j),
                  _layer((GLA_HK, LANES), j),
                  _layer((1, GLA_HK), j),
                  _layer((GLA_HK, 1), j),
                  _layer((1, GLA_DV), j)] + prev_spec,
        out_specs=(pl.BlockSpec((1, DEC_PAD, GLA_HV), lambda b: (b, 0, 0)),
                   pl.BlockSpec(state_block, lambda b: (j, b, 0, 0, 0))),
        input_output_aliases=alias,
        compiler_params=_params(("arbitrary",)),
        name="gla_sample",
    )(qk_s, vgt_s, state_all, wa2_all, wa2t_all, ba_all, bat_all, gn_all, *prev_arg)


def _final_norm_kernel(x_ref, g_ref, yp_ref, ys_ref):
    y = _rms(x_ref[...], g_ref[...])
    yp_ref[...] = y

    @pl.when(pl.program_id(0) == M_ALL // BM - 1)
    def _():
        ys_ref[...] = y[BM - M_SAMPLE:]


def _final_norm(x, g):
    assert M_SAMPLE <= BM
    return pl.pallas_call(
        _final_norm_kernel,
        out_shape=(jax.ShapeDtypeStruct((M_PROMPT, D_MODEL), F32),
                   jax.ShapeDtypeStruct((M_SAMPLE, D_MODEL), F32)),
        grid=(M_ALL // BM,),
        in_specs=[pl.BlockSpec((BM, D_MODEL), lambda i: (i, 0)), _resident((1, D_MODEL))],
        out_specs=(pl.BlockSpec((BM, D_MODEL), lambda i: (i, 0)),
                   pl.BlockSpec((M_SAMPLE, D_MODEL), lambda i: (0, 0))),
        compiler_params=_params(("arbitrary",)),
        name="final_norm",
    )(x, g)


def _rope_tables():
    half = HEAD_DIM // 2
    inv = ROPE_THETA ** (-jnp.arange(half, dtype=F32) * 2.0 / HEAD_DIM)
    pos = jnp.concatenate([
        jnp.tile(jnp.arange(SEQ, dtype=F32), BATCH),
        jnp.tile((PAST_LEN + jnp.arange(DEC_SEQ)).astype(F32), DEC_BATCH)])
    ang = pos[:, None] * inv[None, :]
    cos = jnp.tile(jnp.cos(ang), (1, LANES // half))
    sin = jnp.sin(ang)
    sin = jnp.tile(jnp.concatenate([-sin, sin], axis=1), (1, LANES // HEAD_DIM))
    return cos, sin


def _pad_sample(a):
    a = a.reshape(DEC_BATCH, DEC_SEQ, a.shape[-1])
    return jnp.pad(a, ((0, 0), (0, DEC_PAD - DEC_SEQ), (0, 0)))


def kernel(x_prompt, x_sample, cache_swa_k, cache_swa_v, state_gla, norm_attn, norm_mlp, norm_final, swa_w_qkv, swa_sink, swa_w_o, gla_w_in, gla_w_a1, gla_w_a2, gla_b_a, gla_norm, gla_w_o, mlp_w_up, mlp_w_down):
    x = jnp.concatenate([x_prompt.reshape(M_PROMPT, D_MODEL), x_sample.reshape(M_SAMPLE, D_MODEL)])
    cos, sin = _rope_tables()
    w_qkv = swa_w_qkv.astype(BF16)
    w_swa_o = swa_w_o.astype(BF16)
    w_in = gla_w_in.astype(BF16)
    w_a1 = jnp.pad(gla_w_a1, ((0, 0), (0, 0), (0, LANES - GLA_RANK))).astype(BF16)
    w_a2 = jnp.pad(gla_w_a2, ((0, 0), (0, LANES - GLA_RANK), (0, 0))).astype(BF16)
    w_a2t = jnp.swapaxes(w_a2, 1, 2)
    w_gla_o = gla_w_o.astype(BF16)
    w_up = mlp_w_up.astype(BF16)
    w_down = mlp_w_down.astype(BF16)
    b_a = gla_b_a.reshape(N_GLA, 1, GLA_HK)
    b_a_t = gla_b_a.reshape(N_GLA, GLA_HK, 1)
    g_norm = gla_norm.reshape(N_GLA, 1, GLA_DV)
    cache_k = cache_swa_k.reshape(N_SWA, DEC_BATCH * CACHE_W, NKV)
    cache_v = cache_swa_v.reshape(N_SWA, DEC_BATCH * CACHE_W, NKV)

    kp_l, vp_l, ks_l, vs_l = [], [], [], []
    state_p = state_s = None
    for i in range(DEPTH):
        j = i // 2
        g_attn = norm_attn[i].reshape(1, D_MODEL)
        if i % 2 == 0:
            q, kv = _swa_qkv(x, g_attn, w_qkv, j, cos, sin)
            o = _swa_prompt(swa_sink, j, q, kv)
            o_s = _swa_sample(swa_sink, j, q, kv, cache_k, cache_v)
            o = lax.dynamic_update_slice(o, o_s, (M_PROMPT, 0))
            x = _oproj(o, w_swa_o, j, x)
            kv_p = kv[:M_PROMPT].reshape(BATCH, SEQ, 2 * NKV)[:, SEQ - CACHE_W:]
            kv_p = kv_p.reshape(BATCH, CACHE_W, 2, N_KV, HEAD_DIM)
            kp_l.append(kv_p[:, :, 0])
            vp_l.append(kv_p[:, :, 1])
            kv_s = kv[M_PROMPT:].reshape(DEC_BATCH, DEC_SEQ, 2, N_KV, HEAD_DIM)
            ks_l.append(jnp.concatenate([cache_swa_k[j][:, DEC_SEQ:], kv_s[:, :, 0]], axis=1))
            vs_l.append(jnp.concatenate([cache_swa_v[j][:, DEC_SEQ:], kv_s[:, :, 1]], axis=1))
        else:
            qk, vgt = _gla_proj(x, g_attn, w_in, w_a1, j)
            o, state_p = _gla_prompt(qk, vgt, w_a2, b_a, g_norm, j, state_p)
            o_s, state_s = _gla_sample(
                _pad_sample(qk[M_PROMPT:]), _pad_sample(vgt[M_PROMPT:].astype(F32)),
                state_gla, w_a2, w_a2t, b_a, b_a_t, g_norm, j, state_s)
            o_s = o_s[:, :DEC_SEQ].reshape(M_SAMPLE, GLA_HV).astype(BF16)
            o = lax.dynamic_update_slice(o, o_s, (M_PROMPT, 0))
            x = _oproj(o, w_gla_o, j, x)
        x = _mlp(x, norm_mlp[i].reshape(1, D_MODEL), w_up, w_down, i)
    y_p, y_s = _final_norm(x, norm_final.reshape(1, D_MODEL))
    return (y_p.reshape(BATCH, SEQ, D_MODEL),
            y_s.reshape(DEC_BATCH, DEC_SEQ, D_MODEL),
            jnp.stack(kp_l), jnp.stack(vp_l), state_p,
            jnp.stack(ks_l), jnp.stack(vs_l), state_s)
```

```python
import functools

import jax
import jax.numpy as jnp
from jax import lax
from jax.experimental import pallas as pl
from jax.experimental.pallas import tpu as pltpu

F32 = jnp.float32
BF16 = jnp.bfloat16

D_MODEL = 2048
BATCH = 4
SEQ = 2048
DEPTH = 4
DEC_BATCH = 32
DEC_SEQ = 4
PAST_LEN = 16384
HEAD_DIM = 64
N_HEADS = D_MODEL // HEAD_DIM
N_KV = N_HEADS // 8
GROUP = N_HEADS // N_KV
WINDOW = 128
CACHE_W = WINDOW
ROPE_THETA = 10000.0
ATTN_SCALE = HEAD_DIM ** -0.5
GLA_HEADS = 4
GLA_DK = D_MODEL // 2 // GLA_HEADS
GLA_DV = D_MODEL // GLA_HEADS
GLA_RANK = 16
GLA_TAU = 16.0
GLA_CHUNK = 32
D_FF = 4 * D_MODEL
EPS = 1e-6
N_SWA = (DEPTH + 1) // 2
N_GLA = DEPTH // 2

M_PROMPT = BATCH * SEQ
M_SAMPLE = DEC_BATCH * DEC_SEQ
M_ALL = M_PROMPT + M_SAMPLE

NQ = N_HEADS * HEAD_DIM
NKV = N_KV * HEAD_DIM
GLA_HK = GLA_HEADS * GLA_DK
GLA_HV = GLA_HEADS * GLA_DV
GLA_NIN = 2 * GLA_HK + 2 * GLA_HV
GLA_NVGT = 2 * GLA_HV + 128

LANES = 128
SUBLANES = 8
VMEM_LIMIT = 56 * 1024 * 1024

BM = 640
NCH = 512
BF = 1024
DEC_PAD = SUBLANES
SAMPLE_GROUP = 8
GLA_SUPER = 128

assert M_ALL % BM == 0 and D_FF % BF == 0 and D_MODEL % NCH == 0
assert HEAD_DIM * 2 == LANES and GLA_NVGT == 2 * GLA_HV + LANES
assert GLA_SUPER % GLA_CHUNK == 0 and SEQ % GLA_SUPER == 0


def _params(sem, vmem=VMEM_LIMIT):
    return pltpu.CompilerParams(dimension_semantics=sem, vmem_limit_bytes=vmem)


def _resident(shape):
    return pl.BlockSpec(shape, lambda *_: (0,) * len(shape), pipeline_mode=pl.Buffered(1))


def _layer(shape, j):
    return pl.BlockSpec((None,) + tuple(shape), lambda *_: (j,) + (0,) * len(shape),
                        pipeline_mode=pl.Buffered(1))


def _cast_io(src_all, layer, block, index_map):
    in_spec = pl.BlockSpec((None,) + tuple(block), lambda *g: (layer,) + tuple(index_map(*g)))
    out_spec = pl.BlockSpec(tuple(block), lambda *g: tuple(index_map(*g)))
    return in_spec, out_spec, jax.ShapeDtypeStruct(src_all.shape[1:], BF16)


def _run_casts(srcs, dsts, active):
    for src, dst, act in zip(srcs, dsts, active):
        def convert(src=src, dst=dst):
            dst[...] = src[...].astype(BF16)
        if act is None:
            convert()
        else:
            pl.when(act)(convert)


def _rms(x, g):
    return x * lax.rsqrt(jnp.mean(x * x, axis=-1, keepdims=True) + EPS) * g


def _log_sigmoid(z):
    return jnp.minimum(z, 0.0) - jnp.log1p(jnp.exp(-jnp.abs(z)))


def _dot(a, b):
    return jnp.dot(a, b, preferred_element_type=F32)


def _dot_nt(a, b):
    return lax.dot_general(a, b, (((1,), (1,)), ((), ())), preferred_element_type=F32)


def _dot_tn(a, b):
    return lax.dot_general(a, b, (((0,), (0,)), ((), ())), preferred_element_type=F32)


def _dot_exact_lhs(t, x):
    hi = x.astype(BF16)
    r1 = x - hi.astype(F32)
    mid = r1.astype(BF16)
    lo = (r1 - mid.astype(F32)).astype(BF16)
    return _dot(t, hi) + _dot(t, mid) + _dot(t, lo)


def _swa_qkv_kernel(*refs, split):
    if split:
        xp_ref, xs_ref, g_ref, w_ref, cos_ref, sin_ref, q_ref, kv_ref, xo_ref, h_scr = refs
        last = M_ALL // BM - 1
        n_p = BM - M_SAMPLE

        @pl.when(pl.program_id(0) < last)
        def _():
            x = xp_ref[...]
            xo_ref[...] = x
            h_scr[...] = _rms(x, g_ref[...]).astype(BF16)

        @pl.when(pl.program_id(0) == last)
        def _():
            xa = xp_ref[:n_p]
            xb = xs_ref[...]
            xo_ref[:n_p] = xa
            xo_ref[n_p:] = xb
            h_scr[:n_p] = _rms(xa, g_ref[...]).astype(BF16)
            h_scr[n_p:] = _rms(xb, g_ref[...]).astype(BF16)

        h = h_scr[...]
    else:
        x_ref, g_ref, w_ref, cos_ref, sin_ref, q_ref, kv_ref = refs
        h = _rms(x_ref[...], g_ref[...]).astype(BF16)
    cos = cos_ref[...]
    sin = sin_ref[...]
    lane = lax.broadcasted_iota(jnp.int32, cos.shape, 1)
    first_half = (lane & (HEAD_DIM - 1)) < HEAD_DIM // 2

    def rope(a):
        partner = jnp.where(first_half,
                            pltpu.roll(a, LANES - HEAD_DIM // 2, 1),
                            pltpu.roll(a, HEAD_DIM // 2, 1))
        return a * cos + partner * sin

    for c in range(0, NQ, NCH):
        acc = _dot(h, w_ref[:, c:c + NCH])
        for j in range(0, NCH, LANES):
            q_ref[:, c + j:c + j + LANES] = (rope(acc[:, j:j + LANES]) * ATTN_SCALE).astype(BF16)
    acc = _dot(h, w_ref[:, NQ:NQ + 2 * NKV])
    for j in range(0, NKV, LANES):
        kv_ref[:, j:j + LANES] = rope(acc[:, j:j + LANES])
    kv_ref[:, NKV:] = acc[:, NKV:]


def _swa_qkv(xs, g, w, cos, sin):
    split = len(xs) == 2
    row_block = pl.BlockSpec((BM, D_MODEL), lambda i: (i, 0))
    x_specs = [row_block, _resident((M_SAMPLE, D_MODEL))] if split else [row_block]
    out_shape = [jax.ShapeDtypeStruct((M_ALL, NQ), BF16),
                 jax.ShapeDtypeStruct((M_ALL, 2 * NKV), F32)]
    out_specs = [pl.BlockSpec((BM, NQ), lambda i: (i, 0)),
                 pl.BlockSpec((BM, 2 * NKV), lambda i: (i, 0))]
    if split:
        assert M_SAMPLE <= BM
        out_shape.append(jax.ShapeDtypeStruct((M_ALL, D_MODEL), F32))
        out_specs.append(row_block)
    return pl.pallas_call(
        functools.partial(_swa_qkv_kernel, split=split),
        out_shape=tuple(out_shape),
        grid=(M_ALL // BM,),
        in_specs=x_specs + [_resident((1, D_MODEL)),
                            _resident((D_MODEL, NQ + 2 * NKV)),
                            pl.BlockSpec((BM, LANES), lambda i: (i, 0)),
                            pl.BlockSpec((BM, LANES), lambda i: (i, 0))],
        out_specs=tuple(out_specs),
        scratch_shapes=[pltpu.VMEM((BM, D_MODEL), BF16)] if split else [],
        compiler_params=_params(("arbitrary",)),
        name="swa_qkv",
    )(*xs, g, w, cos, sin)


def _half_lane_pair(tile, in_high):
    lane = lax.broadcasted_iota(jnp.int32, tile.shape, 1)
    other = pltpu.roll(tile, HEAD_DIM, 1)
    zero = jnp.zeros_like(tile)
    low = jnp.where(lane < HEAD_DIM, other if in_high else tile, zero)
    high = jnp.where(lane >= HEAD_DIM, tile if in_high else other, zero)
    return low, high


def _swa_prompt_kernel(sink_ref, q_ref, kvc_ref, kvp_ref, *rest, n_cast):
    o_ref = rest[n_cast]
    _run_casts(rest[:n_cast], rest[n_cast + 1:], [None] * n_cast)
    n = pl.program_id(1)
    pairs = GROUP // 2
    rows = pairs * WINDOW
    row = lax.broadcasted_iota(jnp.int32, (rows, 2 * WINDOW), 0) & (WINDOW - 1)
    col = lax.broadcasted_iota(jnp.int32, (rows, 2 * WINDOW), 1)
    valid = (col > row) & (col <= row + WINDOW) & ((col >= WINDOW) | (n > 0))
    kvcat = jnp.concatenate([kvp_ref[...], kvc_ref[...]], axis=0)
    for kh in range(N_KV):
        t = kh // 2
        k_lo, k_hi = _half_lane_pair(kvcat[:, t * LANES:(t + 1) * LANES], kh % 2 == 1)
        v_lo, v_hi = _half_lane_pair(kvcat[:, NKV + t * LANES:NKV + (t + 1) * LANES], kh % 2 == 1)
        q = jnp.concatenate([q_ref[:, (kh * pairs + p) * LANES:(kh * pairs + p + 1) * LANES]
                             for p in range(pairs)], axis=0)
        acc = [None] * pairs
        for half, (k, v) in enumerate(((k_lo, v_lo), (k_hi, v_hi))):
            s = jnp.where(valid, _dot_nt(q, k.astype(BF16)), -jnp.inf)
            es, rds = [], []
            for p in range(pairs):
                sp = s[p * WINDOW:(p + 1) * WINDOW]
                sk = sink_ref[kh * GROUP + 2 * p + half]
                m = jnp.max(jnp.maximum(sp, sk), axis=-1, keepdims=True)
                e = jnp.exp(sp - m)
                es.append(e.astype(BF16))
                rds.append(1.0 / (jnp.sum(e, axis=-1, keepdims=True) + jnp.exp(sk - m)))
            pv = _dot(jnp.concatenate(es, axis=0), v.astype(BF16))
            for p in range(pairs):
                part = pv[p * WINDOW:(p + 1) * WINDOW] * rds[p]
                acc[p] = part if acc[p] is None else acc[p] + part
        for p in range(pairs):
            c = (kh * pairs + p) * LANES
            o_ref[:, c:c + LANES] = acc[p].astype(BF16)


def _swa_prompt(sink_all, j, q, kv, casts):
    nb = SEQ // WINDOW
    steps = BATCH * nb
    cast_io = [_cast_io(w, layer, (w.shape[1] // steps, w.shape[2]), lambda b, n: (b * nb + n, 0))
               for w, layer in casts]
    return pl.pallas_call(
        functools.partial(_swa_prompt_kernel, n_cast=len(casts)),
        out_shape=(jax.ShapeDtypeStruct((M_ALL, NQ), BF16),) + tuple(c[2] for c in cast_io),
        grid=(BATCH, nb),
        in_specs=[pl.BlockSpec(memory_space=pltpu.SMEM),
                  pl.BlockSpec((WINDOW, NQ), lambda b, n: (b * nb + n, 0)),
                  pl.BlockSpec((WINDOW, 2 * NKV), lambda b, n: (b * nb + n, 0)),
                  pl.BlockSpec((WINDOW, 2 * NKV), lambda b, n: (b * nb + jnp.maximum(n - 1, 0), 0))]
        + [c[0] for c in cast_io],
        out_specs=(pl.BlockSpec((WINDOW, NQ), lambda b, n: (b * nb + n, 0)),)
        + tuple(c[1] for c in cast_io),
        compiler_params=_params(("arbitrary", "arbitrary")),
        name="swa_prompt_attn",
    )(sink_all[j], q, kv, kv, *[w for w, _ in casts])


def _swa_sample_kernel(sink_ref, q_ref, kvn_ref, ck_ref, cv_ref, o_ref):
    rows_q = SAMPLE_GROUP * DEC_SEQ
    n_rows = GROUP * rows_q
    n_cache = SAMPLE_GROUP * CACHE_W
    r = lax.broadcasted_iota(jnp.int32, (n_rows, 1), 0) & (rows_q - 1)
    bq = r >> 2
    tq = r & (DEC_SEQ - 1)
    cc = lax.broadcasted_iota(jnp.int32, (1, n_cache), 1)
    valid_c = ((cc >> 7) == bq) & ((cc & (CACHE_W - 1)) > tq)
    cn = lax.broadcasted_iota(jnp.int32, (1, rows_q), 1)
    valid_n = ((cn >> 2) == bq) & ((cn & (DEC_SEQ - 1)) <= tq)
    kvn = kvn_ref[...].astype(BF16)
    for kh in range(N_KV):
        ks = slice(kh * HEAD_DIM, (kh + 1) * HEAD_DIM)
        vs = slice(NKV + kh * HEAD_DIM, NKV + (kh + 1) * HEAD_DIM)
        ck = ck_ref[:, ks].astype(BF16)
        cv = cv_ref[:, ks].astype(BF16)
        q = jnp.concatenate(
            [q_ref[:, (kh * GROUP + g) * HEAD_DIM:(kh * GROUP + g + 1) * HEAD_DIM]
             for g in range(GROUP)], axis=0)
        sk = jnp.concatenate(
            [jnp.full((rows_q, 1), sink_ref[kh * GROUP + g], F32) for g in range(GROUP)], axis=0)
        s_c = jnp.where(valid_c, _dot_nt(q, ck), -jnp.inf)
        s_n = jnp.where(valid_n, _dot_nt(q, kvn[:, ks]), -jnp.inf)
        m = jnp.maximum(jnp.maximum(jnp.max(s_c, axis=-1, keepdims=True),
                                    jnp.max(s_n, axis=-1, keepdims=True)), sk)
        p_c = jnp.exp(s_c - m)
        p_n = jnp.exp(s_n - m)
        denom = (jnp.sum(p_c, axis=-1, keepdims=True) + jnp.sum(p_n, axis=-1, keepdims=True)
                 + jnp.exp(sk - m))
        o = _dot((p_c / denom).astype(BF16), cv) + _dot((p_n / denom).astype(BF16), kvn[:, vs])
        for g in range(GROUP):
            h = kh * GROUP + g
            o_ref[:, h * HEAD_DIM:(h + 1) * HEAD_DIM] = o[g * rows_q:(g + 1) * rows_q].astype(BF16)


def _swa_sample(sink_all, j, q, kv, cache_k, cache_v):
    rows_q = SAMPLE_GROUP * DEC_SEQ
    n_cache = SAMPLE_GROUP * CACHE_W
    first = M_PROMPT // rows_q
    return pl.pallas_call(
        _swa_sample_kernel,
        out_shape=jax.ShapeDtypeStruct((M_SAMPLE, NQ), BF16),
        grid=(DEC_BATCH // SAMPLE_GROUP,),
        in_specs=[pl.BlockSpec(memory_space=pltpu.SMEM),
                  pl.BlockSpec((rows_q, NQ), lambda i: (first + i, 0)),
                  pl.BlockSpec((rows_q, 2 * NKV), lambda i: (first + i, 0)),
                  pl.BlockSpec((None, n_cache, NKV), lambda i: (j, i, 0)),
                  pl.BlockSpec((None, n_cache, NKV), lambda i: (j, i, 0))],
        out_specs=pl.BlockSpec((rows_q, NQ), lambda i: (i, 0)),
        compiler_params=_params(("arbitrary",)),
        name="swa_sample_attn",
    )(sink_all[j], q, kv, cache_k, cache_v)


def _oproj_kernel(a_ref, w_ref, x_ref, o_ref):
    a = a_ref[...]
    for c in range(0, D_MODEL, NCH):
        o_ref[:, c:c + NCH] = x_ref[:, c:c + NCH] + _dot(a, w_ref[:, c:c + NCH])


def _oproj(a, w, x):
    return pl.pallas_call(
        _oproj_kernel,
        out_shape=jax.ShapeDtypeStruct((M_ALL, D_MODEL), F32),
        grid=(M_ALL // BM,),
        in_specs=[pl.BlockSpec((BM, D_MODEL), lambda i: (i, 0)),
                  _resident((D_MODEL, D_MODEL)),
                  pl.BlockSpec((BM, D_MODEL), lambda i: (i, 0))],
        out_specs=pl.BlockSpec((BM, D_MODEL), lambda i: (i, 0)),
        input_output_aliases={2: 0},
        compiler_params=_params(("arbitrary",)),
        name="out_proj",
    )(a, w, x)


MLP_NF = D_FF // BF
MLP_CAST_ROWS = D_MODEL // MLP_NF


def _mlp_kernel(*refs, n_cast, cast_rows, final):
    x_ref, g_ref, wu_ref, wd_ref = refs[:4]
    k = 4
    if final:
        gf_ref = refs[k]
        k += 1
    srcs = refs[k:k + n_cast]
    k += n_cast
    if final:
        yp_ref, ys_ref = refs[k:k + 2]
        k += 2
    else:
        acc_ref = refs[k]
        k += 1
    dsts = refs[k:k + n_cast]
    k += n_cast
    h_scr = refs[k]
    if final:
        acc_ref = refs[k + 1]
    i = pl.program_id(0)
    f = pl.program_id(1)

    @pl.when(f == 0)
    def _():
        x = x_ref[...]
        h_scr[...] = _rms(x, g_ref[...]).astype(BF16)
        acc_ref[...] = x

    a = jnp.square(jnp.maximum(_dot(h_scr[...], wu_ref[...]), 0.0)).astype(BF16)
    acc_ref[...] += _dot(a, wd_ref[...])

    if final:
        @pl.when(f == MLP_NF - 1)
        def _():
            y = _rms(acc_ref[...], gf_ref[...])
            yp_ref[...] = y

            @pl.when(i == M_ALL // BM - 1)
            def _():
                ys_ref[...] = y[BM - M_SAMPLE:]

    _run_casts(srcs, dsts, [(i >= lo) & (i < hi) for lo, hi in cast_rows])


def _mlp(x, g, w_up, w_down, casts=(), final_gain=None):
    final = final_gain is not None
    cast_io, cast_rows = [], []
    for w, layer, n_col, first in casts:
        assert first + n_col <= M_ALL // BM and w.shape[1] % MLP_NF == 0 and w.shape[2] % n_col == 0
        block = (w.shape[1] // MLP_NF, w.shape[2] // n_col)

        def imap(i, f, n_col=n_col, first=first):
            c = jnp.clip(i - first, 0, n_col - 1)
            r = jnp.where(i < first, 0, jnp.where(i >= first + n_col, MLP_NF - 1, f))
            return r, c
        cast_io.append(_cast_io(w, layer, block, imap))
        cast_rows.append((first, first + n_col))
    row_block = pl.BlockSpec((BM, D_MODEL), lambda i, f: (i, 0))
    if final:
        assert M_SAMPLE <= BM
        out_shape = (jax.ShapeDtypeStruct((M_PROMPT, D_MODEL), F32),
                     jax.ShapeDtypeStruct((M_SAMPLE, D_MODEL), F32))
        out_specs = (row_block, pl.BlockSpec((M_SAMPLE, D_MODEL), lambda i, f: (0, 0)))
        scratch = [pltpu.VMEM((BM, D_MODEL), BF16), pltpu.VMEM((BM, D_MODEL), F32)]
        extra_in, extra_spec, alias = [final_gain], [_resident((1, D_MODEL))], {}
    else:
        out_shape = (jax.ShapeDtypeStruct((M_ALL, D_MODEL), F32),)
        out_specs = (row_block,)
        scratch = [pltpu.VMEM((BM, D_MODEL), BF16)]
        extra_in, extra_spec, alias = [], [], {0: 0}
    return pl.pallas_call(
        functools.partial(_mlp_kernel, n_cast=len(casts), cast_rows=tuple(cast_rows), final=final),
        out_shape=out_shape + tuple(c[2] for c in cast_io),
        grid=(M_ALL // BM, MLP_NF),
        in_specs=[row_block,
                  _resident((1, D_MODEL)),
                  pl.BlockSpec((D_MODEL, BF), lambda i, f: (0, f)),
                  pl.BlockSpec((BF, D_MODEL), lambda i, f: (f, 0))]
        + extra_spec + [c[0] for c in cast_io],
        out_specs=out_specs + tuple(c[1] for c in cast_io),
        scratch_shapes=scratch,
        input_output_aliases=alias,
        compiler_params=_params(("arbitrary", "arbitrary")),
        name="mlp",
    )(x, g, w_up, w_down, *extra_in, *[c[0] for c in casts])


def _gla_proj_kernel(x_ref, g_ref, w_ref, wa1_ref, qk_ref, vgt_ref):
    h = _rms(x_ref[...], g_ref[...]).astype(BF16)
    for c in range(0, 2 * GLA_HK, NCH):
        qk_ref[:, c:c + NCH] = _dot(h, w_ref[:, c:c + NCH])
    for c in range(0, 2 * GLA_HV, NCH):
        vgt_ref[:, c:c + NCH] = _dot(h, w_ref[:, 2 * GLA_HK + c:2 * GLA_HK + c + NCH]).astype(BF16)
    vgt_ref[:, 2 * GLA_HV:] = _dot(h, wa1_ref[...]).astype(BF16)


def _gla_proj(x, g, w, wa1_all, j):
    bm = BM // 2
    return pl.pallas_call(
        _gla_proj_kernel,
        out_shape=(jax.ShapeDtypeStruct((M_ALL, 2 * GLA_HK), F32),
                   jax.ShapeDtypeStruct((M_ALL, GLA_NVGT), BF16)),
        grid=(M_ALL // bm,),
        in_specs=[pl.BlockSpec((bm, D_MODEL), lambda i: (i, 0)),
                  _resident((1, D_MODEL)),
                  _resident((D_MODEL, GLA_NIN)),
                  _layer((D_MODEL, LANES), j)],
        out_specs=(pl.BlockSpec((bm, 2 * GLA_HK), lambda i: (i, 0)),
                   pl.BlockSpec((bm, GLA_NVGT), lambda i: (i, 0))),
        compiler_params=_params(("arbitrary",)),
        name="gla_proj",
    )(x, g, w, wa1_all)


def _gate(o, gn, g):
    on = o * lax.rsqrt(jnp.mean(o * o, axis=-1, keepdims=True) + EPS) * gn
    return on * (g * (1.0 / (1.0 + jnp.exp(-g))))


def _gla_prompt_kernel(q_ref, k_ref, v_ref, g_ref, t_ref, wa2_ref, ba_ref, gn_ref, wo_ref, *rest):
    o_ref, st_ref, wo_bf_ref, la_scr, s_scr = rest[-5:]
    wo_bf_ref[...] = wo_ref[...].astype(BF16)
    c = GLA_CHUNK
    sc = GLA_SUPER
    nsub = sc // c
    z = _dot(t_ref[...], wa2_ref[...]) + ba_ref[...]
    la_scr[...] = _log_sigmoid(z) * (1.0 / GLA_TAU)
    s_scr[...] = jnp.zeros_like(s_scr)
    row = lax.broadcasted_iota(jnp.int32, (c, sc), 0)
    col = lax.broadcasted_iota(jnp.int32, (c, sc), 1)
    tr = lax.broadcasted_iota(jnp.int32, (sc, sc), 0)
    tc = lax.broadcasted_iota(jnp.int32, (sc, sc), 1)
    tri = (tc <= tr).astype(BF16)
    gn = gn_ref[...]

    def block(n, carry):
        r0 = pl.multiple_of(n * sc, sc)
        rows = pl.ds(r0, sc)
        gcum = _dot_exact_lhs(tri, la_scr[rows, :])
        gtot = gcum[sc - 1:sc]
        qs = q_ref[rows, :] * (GLA_DK ** -0.5)
        k = k_ref[rows, :]
        v = v_ref[rows, :]
        s_t = s_scr[...]
        o = _dot_nt((qs * jnp.exp(gcum)).astype(BF16), s_t.astype(BF16))
        slabs = []
        for i in range(nsub):
            lo, hi = i * c, (i + 1) * c
            piv = gcum[lo - 1:lo] if i > 0 else jnp.zeros((1, GLA_DK), F32)
            qe = (qs[lo:hi] * jnp.exp(gcum[lo:hi] - piv)).astype(BF16)
            ke = (k[:hi] * jnp.exp(piv - gcum[:hi])).astype(BF16)
            if hi < sc:
                ke = jnp.concatenate([ke, jnp.zeros((sc - hi, GLA_DK), BF16)], axis=0)
            slabs.append(jnp.where(col <= row + lo, _dot_nt(qe, ke), 0.0))
        att = jnp.concatenate(slabs, axis=0).astype(BF16)
        o = o + _dot(att, v)
        s_scr[...] = s_t * jnp.exp(gtot) + _dot_tn(v, (k * jnp.exp(gtot - gcum)).astype(BF16))
        o_ref[rows, :] = _gate(o, gn, g_ref[rows, :].astype(F32)).astype(BF16)
        return carry

    lax.fori_loop(0, SEQ // sc, block, 0, unroll=4)
    st_ref[0, 0] = s_scr[...].T


def _stacked_alias(prev, n_inputs, out_index):
    if prev is None:
        return [], [], {}
    return [pl.BlockSpec(memory_space=pl.ANY)], [prev], {n_inputs: out_index}


def _gla_prompt(qk, vgt, wa2_all, ba_all, gn_all, wo_all, j, prev_state):
    kb = GLA_HEADS
    gb = GLA_HEADS
    tb = 2 * GLA_HV // LANES
    prev_spec, prev_arg, alias = _stacked_alias(prev_state, 9, 1)
    wo_in, wo_out, wo_shape = _cast_io(
        wo_all, j, (GLA_HV // (BATCH * GLA_HEADS), D_MODEL), lambda b, h: (b * GLA_HEADS + h, 0))
    return pl.pallas_call(
        _gla_prompt_kernel,
        out_shape=(jax.ShapeDtypeStruct((M_ALL, GLA_HV), BF16),
                   jax.ShapeDtypeStruct((N_GLA, BATCH, GLA_HEADS, GLA_DK, GLA_DV), F32),
                   wo_shape),
        grid=(BATCH, GLA_HEADS),
        in_specs=[pl.BlockSpec((SEQ, GLA_DK), lambda b, h: (b, h)),
                  pl.BlockSpec((SEQ, GLA_DK), lambda b, h: (b, kb + h)),
                  pl.BlockSpec((SEQ, GLA_DV), lambda b, h: (b, h)),
                  pl.BlockSpec((SEQ, GLA_DV), lambda b, h: (b, gb + h)),
                  pl.BlockSpec((SEQ, LANES), lambda b, h: (b, tb)),
                  pl.BlockSpec((None, LANES, GLA_DK), lambda b, h: (j, 0, h)),
                  pl.BlockSpec((None, 1, GLA_DK), lambda b, h: (j, 0, h)),
                  _layer((1, GLA_DV), j),
                  wo_in] + prev_spec,
        out_specs=(pl.BlockSpec((SEQ, GLA_DV), lambda b, h: (b, h)),
                   pl.BlockSpec((None, 1, 1, GLA_DK, GLA_DV), lambda b, h: (j, b, h, 0, 0)),
                   wo_out),
        scratch_shapes=[pltpu.VMEM((SEQ, GLA_DK), F32),
                        pltpu.VMEM((GLA_DV, GLA_DK), F32)],
        input_output_aliases=alias,
        compiler_params=_params(("arbitrary", "arbitrary")),
        name="gla_prompt",
    )(qk, qk, vgt, vgt, vgt, wa2_all, ba_all, gn_all, wo_all, *prev_arg)


def _gla_sample_kernel(qk_ref, vgt_ref, s0_ref, wa2_ref, wa2t_ref, ba_ref, bat_ref, gn_ref, *rest):
    o_ref, s1_ref = rest[-2:]
    c = DEC_PAD
    row = lax.broadcasted_iota(jnp.int32, (c, c), 0)
    col = lax.broadcasted_iota(jnp.int32, (c, c), 1)
    causal = row >= col
    tri = causal.astype(BF16)
    real_row = lax.broadcasted_iota(jnp.int32, (c, GLA_DK), 0) < DEC_SEQ
    real_col = lax.broadcasted_iota(jnp.int32, (GLA_DK, c), 1) < DEC_SEQ
    t = vgt_ref[0, :, 2 * GLA_HV:].astype(BF16)
    gn = gn_ref[...]
    for h in range(GLA_HEADS):
        ks = slice(h * GLA_DK, (h + 1) * GLA_DK)
        vs = slice(h * GLA_DV, (h + 1) * GLA_DV)
        z = _dot(t, wa2_ref[:, ks]) + ba_ref[:, ks]
        la = jnp.where(real_row, _log_sigmoid(z) * (1.0 / GLA_TAU), 0.0)
        cum = _dot_exact_lhs(tri, la)
        last = cum[c - 1:c, :]
        z_t = _dot_nt(wa2t_ref[ks, :], t) + bat_ref[ks, :]
        la_t = jnp.where(real_col, _log_sigmoid(z_t) * (1.0 / GLA_TAU), 0.0)
        last_t = jnp.sum(la_t, axis=1, keepdims=True)
        kc = qk_ref[0, :, GLA_HK + h * GLA_DK:GLA_HK + (h + 1) * GLA_DK]
        vc = vgt_ref[0, :, vs].astype(BF16)
        qe = (qk_ref[0, :, ks] * (GLA_DK ** -0.5) * jnp.exp(cum)).astype(BF16)
        ke = (kc * jnp.exp(-cum)).astype(BF16)
        kd = (kc * jnp.exp(last - cum)).astype(BF16)
        att = jnp.where(causal, _dot_nt(qe, ke), 0.0).astype(BF16)
        s0 = s0_ref[0, h]
        o = _dot(att, vc) + _dot(qe, s0.astype(BF16))
        s1_ref[0, h] = s0 * jnp.exp(last_t) + _dot_tn(kd, vc)
        o_ref[0, :, vs] = _gate(o, gn, vgt_ref[0, :, GLA_HV + h * GLA_DV:GLA_HV + (h + 1) * GLA_DV])


def _gla_sample(qk_s, vgt_s, state_all, wa2_all, wa2t_all, ba_all, bat_all, gn_all, j, prev_state):
    state_block = (None, 1, GLA_HEADS, GLA_DK, GLA_DV)
    prev_spec, prev_arg, alias = _stacked_alias(prev_state, 8, 1)
    return pl.pallas_call(
        _gla_sample_kernel,
        out_shape=(jax.ShapeDtypeStruct((DEC_BATCH, DEC_PAD, GLA_HV), F32),
                   jax.ShapeDtypeStruct((N_GLA, DEC_BATCH, GLA_HEADS, GLA_DK, GLA_DV), F32)),
        grid=(DEC_BATCH,),
        in_specs=[pl.BlockSpec((1, DEC_PAD, 2 * GLA_HK), lambda b: (b, 0, 0)),
                  pl.BlockSpec((1, DEC_PAD, GLA_NVGT), lambda b: (b, 0, 0)),
                  pl.BlockSpec(state_block, lambda b: (j, b, 0, 0, 0)),
                  _layer((LANES, GLA_HK), j),
                  _layer((GLA_HK, LANES), j),
                  _layer((1, GLA_HK), j),
                  _layer((GLA_HK, 1), j),
                  _layer((1, GLA_DV), j)] + prev_spec,
        out_specs=(pl.BlockSpec((1, DEC_PAD, GLA_HV), lambda b: (b, 0, 0)),
                   pl.BlockSpec(state_block, lambda b: (j, b, 0, 0, 0))),
        input_output_aliases=alias,
        compiler_params=_params(("arbitrary",)),
        name="gla_sample",
    )(qk_s, vgt_s, state_all, wa2_all, wa2t_all, ba_all, bat_all, gn_all, *prev_arg)


def _rope_tables():
    half = HEAD_DIM // 2
    inv = ROPE_THETA ** (-jnp.arange(half, dtype=F32) * 2.0 / HEAD_DIM)
    pos = jnp.concatenate([
        jnp.tile(jnp.arange(SEQ, dtype=F32), BATCH),
        jnp.tile((PAST_LEN + jnp.arange(DEC_SEQ)).astype(F32), DEC_BATCH)])
    ang = pos[:, None] * inv[None, :]
    cos = jnp.tile(jnp.cos(ang), (1, LANES // half))
    sin = jnp.sin(ang)
    sin = jnp.tile(jnp.concatenate([-sin, sin], axis=1), (1, LANES // HEAD_DIM))
    return cos, sin


def _pad_sample(a):
    a = a.reshape(DEC_BATCH, DEC_SEQ, a.shape[-1])
    return jnp.pad(a, ((0, 0), (0, DEC_PAD - DEC_SEQ), (0, 0)))


def kernel(x_prompt, x_sample, cache_swa_k, cache_swa_v, state_gla, norm_attn, norm_mlp, norm_final, swa_w_qkv, swa_sink, swa_w_o, gla_w_in, gla_w_a1, gla_w_a2, gla_b_a, gla_norm, gla_w_o, mlp_w_up, mlp_w_down):
    cos, sin = _rope_tables()
    w_qkv = swa_w_qkv[0].astype(BF16)
    w_a1 = jnp.pad(gla_w_a1, ((0, 0), (0, 0), (0, LANES - GLA_RANK))).astype(BF16)
    w_a2 = jnp.pad(gla_w_a2, ((0, 0), (0, LANES - GLA_RANK), (0, 0))).astype(BF16)
    w_a2t = jnp.swapaxes(w_a2, 1, 2)
    b_a = gla_b_a.reshape(N_GLA, 1, GLA_HK)
    b_a_t = gla_b_a.reshape(N_GLA, GLA_HK, 1)
    g_norm = gla_norm.reshape(N_GLA, 1, GLA_DV)
    cache_k = cache_swa_k.reshape(N_SWA, DEC_BATCH * CACHE_W, NKV)
    cache_v = cache_swa_v.reshape(N_SWA, DEC_BATCH * CACHE_W, NKV)
    n_row_blocks = M_ALL // BM
    mixer_first = MLP_NF
    assert n_row_blocks - mixer_first >= 4

    kp_l, vp_l, ks_l, vs_l = [], [], [], []
    state_p = state_s = None
    xs = (x_prompt.reshape(M_PROMPT, D_MODEL), x_sample.reshape(M_SAMPLE, D_MODEL))
    w_up = w_down = w_in = None
    for i in range(DEPTH):
        j = i // 2
        g_attn = norm_attn[i].reshape(1, D_MODEL)
        if i % 2 == 0:
            q, kv, *x_new = _swa_qkv(xs, g_attn, w_qkv, cos, sin)
            x = x_new[0] if x_new else xs[0]
            casts = [(swa_w_o, j)]
            if i == 0:
                casts += [(mlp_w_up, 0), (mlp_w_down, 0)]
            o, w_o, *w_mlp = _swa_prompt(swa_sink, j, q, kv, casts)
            if i == 0:
                w_up, w_down = w_mlp
            o_s = _swa_sample(swa_sink, j, q, kv, cache_k, cache_v)
            o = lax.dynamic_update_slice(o, o_s, (M_PROMPT, 0))
            x = _oproj(o, w_o, x)
            kv_p = jnp.stack([kv[(b + 1) * SEQ - CACHE_W:(b + 1) * SEQ] for b in range(BATCH)])
            kv_p = kv_p.reshape(BATCH, CACHE_W, 2, N_KV, HEAD_DIM)
            kp_l.append(kv_p[:, :, 0])
            vp_l.append(kv_p[:, :, 1])
            kv_s = kv[M_PROMPT:].reshape(DEC_BATCH, DEC_SEQ, 2, N_KV, HEAD_DIM)
            ks_l.append(jnp.concatenate([cache_swa_k[j][:, DEC_SEQ:], kv_s[:, :, 0]], axis=1))
            vs_l.append(jnp.concatenate([cache_swa_v[j][:, DEC_SEQ:], kv_s[:, :, 1]], axis=1))
        else:
            qk, vgt = _gla_proj(x, g_attn, w_in, w_a1, j)
            o, state_p, w_o = _gla_prompt(qk, vgt, w_a2, b_a, g_norm, gla_w_o, j, state_p)
            o_s, state_s = _gla_sample(
                _pad_sample(qk[M_PROMPT:]), _pad_sample(vgt[M_PROMPT:].astype(F32)),
                state_gla, w_a2, w_a2t, b_a, b_a_t, g_norm, j, state_s)
            o_s = o_s[:, :DEC_SEQ].reshape(M_SAMPLE, GLA_HV).astype(BF16)
            o = lax.dynamic_update_slice(o, o_s, (M_PROMPT, 0))
            x = _oproj(o, w_o, x)
        g_mlp = norm_mlp[i].reshape(1, D_MODEL)
        if i == DEPTH - 1:
            y_p, y_s = _mlp(x, g_mlp, w_up, w_down, final_gain=norm_final.reshape(1, D_MODEL))
        else:
            nxt = (i + 1) // 2
            mixer = (gla_w_in, nxt, 4, mixer_first) if (i + 1) % 2 else (swa_w_qkv, nxt, 4, mixer_first)
            x, w_up, w_down, w_mix = _mlp(
                x, g_mlp, w_up, w_down,
                casts=[(mlp_w_up, i + 1, MLP_NF, 0), (mlp_w_down, i + 1, MLP_NF, 0), mixer])
            if (i + 1) % 2:
                w_in = w_mix
            else:
                w_qkv = w_mix
            xs = (x,)
    return (y_p.reshape(BATCH, SEQ, D_MODEL),
            y_s.reshape(DEC_BATCH, DEC_SEQ, D_MODEL),
            jnp.stack(kp_l), jnp.stack(vp_l), state_p,
            jnp.stack(ks_l), jnp.stack(vs_l), state_s)
```
